```python
import functools
import jax, jax.numpy as jnp
from jax import lax
import numpy as np

D_MODEL = 1024
BATCH = 8
SEQ = 2048
DEPTH = 2
DEC_BATCH = 128
DEC_SEQ = 1
PAST_LEN = 16384
PAGE_SIZE = 128

RET_HEADS = 4
RET_DK = 128
RET_DV = 256
RET_CHUNK = 128
RET_QK = RET_HEADS * RET_DK
RET_V = RET_HEADS * RET_DV
MLA_HEADS = 16
NOPE_DIM = 64
ROPE_DIM = 32
V_DIM = 64
Q_RANK = 384
KV_RANK = 256
QK_DIM = NOPE_DIM + ROPE_DIM
MLA_SCALE = QK_DIM ** -0.5
Q_BLOCK = 128
ROPE_THETA = 10000.0
D_FF = 2816
N_MOD = 9
EPS = 1e-6
IN_SPLITS = (RET_QK, RET_QK, RET_V, RET_V, Q_RANK, KV_RANK, ROPE_DIM, D_MODEL, D_MODEL)
N_IN = RET_QK * 2 + RET_V * 2 + Q_RANK + KV_RANK + ROPE_DIM + 2 * D_MODEL

kernel_name = 'hybrid_retention_mla_macaron_adaln_step'


def _rmsnorm(x, g):
    x32 = x.astype(jnp.float32)
    y = x32 * lax.rsqrt(jnp.mean(x32 * x32, axis=-1, keepdims=True) + EPS)
    return (y * g.astype(jnp.float32)).astype(x.dtype)


def _modulate(x, g, shift, scale):
    return _rmsnorm(x, g) * (1.0 + scale) + shift


def _swiglu(h, w13, w2):
    a, b = jnp.split(h @ w13, 2, axis=-1)
    return (jax.nn.silu(a) * b) @ w2


def _rope(x, pos):
    half = x.shape[-1] // 2
    freqs = ROPE_THETA ** (-jnp.arange(half, dtype=jnp.float32) / half)
    ang = pos[:, None] * freqs[None, :]
    cos, sin = jnp.cos(ang)[:, None, :], jnp.sin(ang)[:, None, :]
    x32 = x.astype(jnp.float32)
    x1, x2 = x32[..., :half], x32[..., half:]
    return jnp.concatenate([x1 * cos - x2 * sin, x2 * cos + x1 * sin], axis=-1).astype(x.dtype)


def _group_norm(o, g):
    o32 = o.astype(jnp.float32)
    mu = jnp.mean(o32, axis=-1, keepdims=True)
    var = jnp.mean((o32 - mu) ** 2, axis=-1, keepdims=True)
    return ((o32 - mu) * lax.rsqrt(var + EPS) * g.astype(jnp.float32)).astype(o.dtype)


def _retention(q, k, v, s0):
    B, L = q.shape[:2]
    C = RET_CHUNK if L % RET_CHUNK == 0 else L
    nc = L // C
    log_gamma = jnp.log1p(-jnp.exp2(-5.0 - jnp.arange(RET_HEADS, dtype=jnp.float32)))
    idx = jnp.arange(C, dtype=jnp.float32)
    rel = idx[:, None] - idx[None, :]
    decay = jnp.where(rel[None] >= 0, jnp.exp(rel[None] * log_gamma[:, None, None]), 0.0)
    xi = jnp.exp((idx[:, None] + 1.0) * log_gamma[None, :])[None, :, :, None]
    zeta = jnp.exp((C - 1.0 - idx)[:, None] * log_gamma[None, :])[None, :, :, None]
    g_chunk = jnp.exp(C * log_gamma)[None, :, None, None]

    def to_chunks(t):
        t = t.astype(jnp.float32)
        return t.reshape(B, nc, C, *t.shape[2:]).swapaxes(0, 1)

    def step(s, inp):
        qc, kc, vc = inp
        scores = jnp.einsum('bnhd,bmhd->bhnm', qc, kc) * decay
        inner = jnp.einsum('bhnm,bmhv->bnhv', scores, vc)
        cross = jnp.einsum('bnhd,bhdv->bnhv', qc, s) * xi
        s_new = g_chunk * s + jnp.einsum('bmhd,bmhv->bhdv', kc * zeta, vc)
        return s_new, inner + cross

    s_final, out = lax.scan(step, s0.astype(jnp.float32), (to_chunks(q), to_chunks(k), to_chunks(v)))
    out = out.swapaxes(0, 1).reshape(B, L, RET_HEADS, RET_DV)
    return out.astype(q.dtype), s_final.astype(s0.dtype)


def _mla_logits(q_lat, q_pe, ckv, kpe, ks):
    s = jnp.einsum('bqhr,bkr->bhqk', q_lat, ckv) + jnp.einsum('bqhp,bkp->bhqk', q_pe, kpe)
    return s.astype(jnp.float32) * jnp.transpose(ks.astype(jnp.float32), (0, 2, 1))[:, :, None, :] * MLA_SCALE


def _mla_attend_prompt(q_lat, q_pe, ckv, kpe, ks):
    B, L = q_lat.shape[:2]
    qb = min(Q_BLOCK, L)
    nb = L // qb
    kpos = jnp.arange(L)

    def block(args):
        i, ql, qp = args
        s = _mla_logits(ql, qp, ckv, kpe, ks)
        qpos = i * qb + jnp.arange(qb)
        s = jnp.where((qpos[:, None] >= kpos[None, :])[None, None], s, -jnp.inf)
        p = jax.nn.softmax(s, axis=-1).astype(ckv.dtype)
        return jnp.einsum('bhqk,bkr->bqhr', p, ckv)

    def to_blocks(t):
        return t.reshape(B, nb, qb, *t.shape[2:]).swapaxes(0, 1)

    o = lax.map(block, (jnp.arange(nb), to_blocks(q_lat), to_blocks(q_pe)))
    return o.swapaxes(0, 1).reshape(B, L, MLA_HEADS, KV_RANK)


def _mla_attend_sample(q_lat, q_pe, ckv, kpe, ks, *, pools, layer, page_table):
    B, L = q_lat.shape[:2]
    pool_ckv, pool_kpe, pool_ks = pools

    def gather(pool):
        return pool[layer, page_table].reshape(B, -1, pool.shape[-1])

    ckv_past = gather(pool_ckv)
    s_past = _mla_logits(q_lat, q_pe, ckv_past, gather(pool_kpe), gather(pool_ks))
    s_new = _mla_logits(q_lat, q_pe, ckv, kpe, ks)
    causal = jnp.tril(jnp.ones((L, L), dtype=bool))
    s_new = jnp.where(causal[None, None], s_new, -jnp.inf)
    n_past = s_past.shape[-1]
    p = jax.nn.softmax(jnp.concatenate([s_past, s_new], axis=-1), axis=-1).astype(ckv.dtype)
    return (jnp.einsum('bhqk,bkr->bqhr', p[..., :n_past], ckv_past)
            + jnp.einsum('bhqk,bkr->bqhr', p[..., n_past:], ckv))


def _token_mixer(h, pos, ret_s0, attend, lw):
    B, L, _ = h.shape
    offs = [int(o) for o in np.cumsum(IN_SPLITS)[:-1]]
    rq, rk, rv, rg, cq, ckv_raw, kpe_raw, ga, gb = jnp.split(h @ lw['w_in'], offs, axis=-1)
    rq = _rope(rq.reshape(B, L, RET_HEADS, RET_DK), pos)
    rk = _rope(rk.reshape(B, L, RET_HEADS, RET_DK), pos) * (RET_DK ** -0.5)
    rv = rv.reshape(B, L, RET_HEADS, RET_DV)
    ret_o, ret_s = _retention(rq, rk, rv, ret_s0)
    ret_o = _group_norm(ret_o, lw['ret_gn_g']).reshape(B, L, RET_V)
    y_a = (ret_o * jax.nn.silu(rg)) @ lw['w_pa']
    c_q = _rmsnorm(cq, lw['mla_qa_g'])
    q = _rmsnorm(jnp.einsum('blr,rhd->blhd', c_q, lw['w_uq']), lw['mla_qn_g'])
    q_nope, q_pe = q[..., :NOPE_DIM], _rope(q[..., NOPE_DIM:], pos)
    c_kv = _rmsnorm(ckv_raw, lw['mla_kva_g'])
    k_nope = jnp.einsum('blr,rhn->blhn', c_kv, lw['w_uk']).astype(jnp.float32)
    kp32 = kpe_raw.astype(jnp.float32)
    k_ms = (jnp.sum(k_nope * k_nope, axis=-1) + jnp.sum(kp32 * kp32, axis=-1)[:, :, None]) / QK_DIM
    k_scale = lax.rsqrt(k_ms + EPS).astype(h.dtype)
    k_pe = _rope((kpe_raw * lw['mla_kn_g'][NOPE_DIM:])[:, :, None, :], pos)[:, :, 0, :]
    q_lat = jnp.einsum('blhn,rhn->blhr', q_nope * lw['mla_kn_g'][:NOPE_DIM], lw['w_uk'])
    o_lat = attend(q_lat, q_pe, c_kv, k_pe, k_scale)
    y_b = jnp.einsum('blhr,rhv->blhv', o_lat, lw['w_uv']).reshape(B, L, MLA_HEADS * V_DIM) @ lw['w_pb']
    merged = jax.nn.sigmoid(ga) * y_a + jax.nn.sigmoid(gb) * y_b
    return merged @ lw['w_out'], (c_kv, k_pe, k_scale, ret_s)


def _trunk_layer(x, c, pos, ret_s0, attend, lw):
    mod = (jax.nn.silu(c) @ lw['ada_w'] + lw['ada_b'])[:, None, :]
    sh1, sc1, g1, sh2, sc2, g2, sh3, sc3, g3 = jnp.split(mod, N_MOD, axis=-1)
    h = _modulate(x, lw['norm_g'][0], sh1, sc1)
    x = x + 0.5 * g1 * _swiglu(h, lw['ffn_w13'][0], lw['ffn_w2'][0])
    h = _modulate(x, lw['norm_g'][1], sh2, sc2)
    mix, new_state = _token_mixer(h, pos, ret_s0, attend, lw)
    x = x + g2 * mix
    h = _modulate(x, lw['norm_g'][2], sh3, sc3)
    x = x + 0.5 * g3 * _swiglu(h, lw['ffn_w13'][1], lw['ffn_w2'][1])
    return x, new_state


def setup_inputs(seed: int = 0) -> dict:
    key = jax.random.key(seed)
    ks = jax.random.split(key, 32)
    f32 = jnp.float32
    n_pages = PAST_LEN // PAGE_SIZE
    n_used = DEC_BATCH * n_pages
    n_pool = n_used + max(1, n_used // 4)

    def nrm(k, shape, scale):
        return jax.random.normal(k, shape, f32) * scale

    def gain(k, shape):
        return 1.0 + 0.05 * jax.random.normal(k, shape, f32)

    page_table = jax.random.permutation(ks[0], n_pool)[:n_used].reshape(DEC_BATCH, n_pages).astype(jnp.int32)
    return {
        'x_prompt': nrm(ks[1], (BATCH, SEQ, D_MODEL), 1.0),
        'x_sample': nrm(ks[2], (DEC_BATCH, DEC_SEQ, D_MODEL), 1.0),
        'cache_ckv': nrm(ks[3], (DEPTH, n_pool, PAGE_SIZE, KV_RANK), 1.0),
        'cache_kpe': nrm(ks[4], (DEPTH, n_pool, PAGE_SIZE, ROPE_DIM), 1.0),
        'cache_kscale': jax.random.uniform(ks[5], (DEPTH, n_pool, PAGE_SIZE, MLA_HEADS), f32, 0.8, 1.2),
        'state_ret': nrm(ks[6], (DEPTH, DEC_BATCH, RET_HEADS, RET_DK, RET_DV), 0.5),
        'page_table': page_table,
        'c_prompt': nrm(ks[7], (BATCH, D_MODEL), 1.0),
        'c_sample': nrm(ks[8], (DEC_BATCH, D_MODEL), 1.0),
        'ada_w': nrm(ks[9], (DEPTH, D_MODEL, N_MOD * D_MODEL), 0.5 * D_MODEL ** -0.5),
        'ada_b': nrm(ks[10], (DEPTH, N_MOD * D_MODEL), 0.01),
        'norm_g': gain(ks[11], (DEPTH, 3, D_MODEL)),
        'ffn_w13': nrm(ks[12], (DEPTH, 2, D_MODEL, 2 * D_FF), D_MODEL ** -0.5),
        'ffn_w2': nrm(ks[13], (DEPTH, 2, D_FF, D_MODEL), D_FF ** -0.5),
        'w_in': nrm(ks[14], (DEPTH, D_MODEL, N_IN), D_MODEL ** -0.5),
        'ret_gn_g': gain(ks[15], (DEPTH, RET_HEADS, RET_DV)),
        'mla_qa_g': gain(ks[16], (DEPTH, Q_RANK)),
        'mla_kva_g': gain(ks[17], (DEPTH, KV_RANK)),
        'w_uq': nrm(ks[18], (DEPTH, Q_RANK, MLA_HEADS, QK_DIM), Q_RANK ** -0.5),
        'mla_qn_g': gain(ks[19], (DEPTH, QK_DIM)),
        'mla_kn_g': gain(ks[20], (DEPTH, QK_DIM)),
        'w_uk': nrm(ks[21], (DEPTH, KV_RANK, MLA_HEADS, NOPE_DIM), KV_RANK ** -0.5),
        'w_uv': nrm(ks[22], (DEPTH, KV_RANK, MLA_HEADS, V_DIM), KV_RANK ** -0.5),
        'w_pa': nrm(ks[23], (DEPTH, RET_V, D_MODEL), RET_V ** -0.5),
        'w_pb': nrm(ks[24], (DEPTH, MLA_HEADS * V_DIM, D_MODEL), (MLA_HEADS * V_DIM) ** -0.5),
        'w_out': nrm(ks[25], (DEPTH, D_MODEL, D_MODEL), D_MODEL ** -0.5),
    }


def reference(x_prompt, x_sample, cache_ckv, cache_kpe, cache_kscale, state_ret, page_table,
              c_prompt, c_sample, ada_w, ada_b, norm_g, ffn_w13, ffn_w2, w_in, ret_gn_g,
              mla_qa_g, mla_kva_g, w_uq, mla_qn_g, mla_kn_g, w_uk, w_uv, w_pa, w_pb, w_out):
    past_len = page_table.shape[1] * PAGE_SIZE
    pos_p = jnp.arange(x_prompt.shape[1], dtype=jnp.float32)
    pos_s = past_len + jnp.arange(x_sample.shape[1], dtype=jnp.float32)
    s0_prompt = jnp.zeros((x_prompt.shape[0], RET_HEADS, RET_DK, RET_DV), x_prompt.dtype)
    pools = (cache_ckv, cache_kpe, cache_kscale)
    yp, ys = x_prompt, x_sample
    st_p, st_s = [], []
    for l in range(DEPTH):
        lw = dict(ada_w=ada_w[l], ada_b=ada_b[l], norm_g=norm_g[l], ffn_w13=ffn_w13[l], ffn_w2=ffn_w2[l],
                  w_in=w_in[l], ret_gn_g=ret_gn_g[l], mla_qa_g=mla_qa_g[l], mla_kva_g=mla_kva_g[l],
                  w_uq=w_uq[l], mla_qn_g=mla_qn_g[l], mla_kn_g=mla_kn_g[l], w_uk=w_uk[l], w_uv=w_uv[l],
                  w_pa=w_pa[l], w_pb=w_pb[l], w_out=w_out[l])
        yp, new_p = _trunk_layer(yp, c_prompt, pos_p, s0_prompt, _mla_attend_prompt, lw)
        attend_s = functools.partial(_mla_attend_sample, pools=pools, layer=l, page_table=page_table)
        ys, new_s = _trunk_layer(ys, c_sample, pos_s, state_ret[l], attend_s, lw)
        st_p.append(new_p)
        st_s.append(new_s)

    def stack(states, i):
        return jnp.stack([st[i] for st in states], axis=0)

    return (yp, ys,
            stack(st_p, 0), stack(st_p, 1), stack(st_p, 2), stack(st_p, 3),
            stack(st_s, 0), stack(st_s, 1), stack(st_s, 2), stack(st_s, 3))
```

```python
import functools

import jax
import jax.numpy as jnp
import numpy as np
from jax import lax
from jax.experimental import pallas as pl
from jax.experimental.pallas import tpu as pltpu

F32 = jnp.float32
BF16 = jnp.bfloat16

RET_HEADS = 4
RET_DK = 128
RET_DV = 256
RET_CHUNK = 128
MLA_HEADS = 16
NOPE_DIM = 64
ROPE_DIM = 32
V_DIM = 64
QK_DIM = NOPE_DIM + ROPE_DIM
MLA_SCALE = QK_DIM ** -0.5
ROPE_THETA = 10000.0
N_MOD = 9
EPS = 1e-6
PAGE_SIZE = 128

LANES = 128
HEAD_SLOT = LANES
VMEM_LIMIT = 56 * 1024 * 1024
NEG_INF = float("-inf")


def _cparams(*sem):
    return pltpu.CompilerParams(dimension_semantics=sem, vmem_limit_bytes=VMEM_LIMIT)


def _dot(a, b):
    return jnp.dot(a, b, preferred_element_type=F32)


def _dot_nt(a, b):
    return lax.dot_general(a, b, (((1,), (1,)), ((), ())), preferred_element_type=F32)


def _dot_tn(a, b):
    return lax.dot_general(a, b, (((0,), (0,)), ((), ())), preferred_element_type=F32)


def _const_spec(shape):
    n = len(shape)
    return pl.BlockSpec(shape, lambda *_: (0,) * n)


def _modulated_norm(x, g, shift, scale):
    ms = jnp.mean(x * x, axis=-1, keepdims=True)
    y = x * lax.rsqrt(ms + EPS) * g
    return y * (1.0 + scale) + shift


def _mod_spec(lm_blk, d, k):
    if lm_blk == 1:
        return pl.BlockSpec((None, 1, d), lambda b, i: (b, 0, k))
    return pl.BlockSpec((None, lm_blk, d), lambda b, i: (b, i, k))


def _mod_kernel(c_ref, w_ref, b_ref, o_ref):
    c = c_ref[...]
    s = (c * jax.nn.sigmoid(c)).astype(BF16)
    o_ref[...] = _dot(s, w_ref[...]) + b_ref[...]


def _mod_call(c_all, ada_w_bf, ada_b):
    depth, d, nd = ada_w_bf.shape
    n = c_all.shape[0]
    return pl.pallas_call(
        _mod_kernel,
        grid=(depth, nd // d),
        in_specs=[
            pl.BlockSpec((n, d), lambda l, j: (0, 0)),
            pl.BlockSpec((None, d, d), lambda l, j: (l, 0, j)),
            pl.BlockSpec((None, 1, d), lambda l, j: (l, 0, j)),
        ],
        out_specs=pl.BlockSpec((None, n, d), lambda l, j: (l, 0, j)),
        out_shape=jax.ShapeDtypeStruct((depth, n, nd), F32),
        compiler_params=_cparams("arbitrary", "arbitrary"),
        name="adaln_mod",
    )(c_all, ada_w_bf, ada_b.reshape(depth, 1, nd))


def _ffn_kernel(x_ref, sh_ref, sc_ref, gt_ref, g_ref, w13_ref, w2_ref, o_ref, *, d_ff):
    x = x_ref[...]
    h = _modulated_norm(x, g_ref[...], sh_ref[...], sc_ref[...]).astype(BF16)
    ab = _dot(h, w13_ref[...])
    a, b = ab[:, :d_ff], ab[:, d_ff:]
    u = (a * jax.nn.sigmoid(a) * b).astype(BF16)
    y = _dot(u, w2_ref[...])
    o_ref[...] = x + 0.5 * gt_ref[...] * y


def _ffn_call(x3, mod3, k0, norm_g, w13, w2, tl):
    bx, lx, d = x3.shape
    lm = mod3.shape[1]
    lm_blk = 1 if lm == 1 else tl
    d_ff = w2.shape[0]
    return pl.pallas_call(
        functools.partial(_ffn_kernel, d_ff=d_ff),
        grid=(bx, lx // tl),
        in_specs=[
            pl.BlockSpec((None, tl, d), lambda b, i: (b, i, 0)),
            _mod_spec(lm_blk, d, k0), _mod_spec(lm_blk, d, k0 + 1), _mod_spec(lm_blk, d, k0 + 2),
            _const_spec((1, d)),
            pl.BlockSpec(w13.shape, lambda b, i: (0, 0), pipeline_mode=pl.Buffered(1)),
            pl.BlockSpec(w2.shape, lambda b, i: (0, 0), pipeline_mode=pl.Buffered(1)),
        ],
        out_specs=pl.BlockSpec((None, tl, d), lambda b, i: (b, i, 0)),
        out_shape=jax.ShapeDtypeStruct(x3.shape, F32),
        compiler_params=_cparams("arbitrary", "arbitrary"),
        name="ffn",
    )(x3, mod3, mod3, mod3, norm_g.reshape(1, d), w13, w2)


def _rope_slot(t, c, sm, sp):
    return t * c + pltpu.roll(t, LANES - ROPE_DIM // 2, axis=1) * sm + pltpu.roll(t, ROPE_DIM // 2, axis=1) * sp


def _inproj_kernel(x_ref, sh_ref, sc_ref, g_ref, w_ref, cosr_ref, sinr_ref, rc_ref, rsm_ref, rsp_ref,
                   gqa_ref, gkva_ref, aq_ref, ak_ref, wuq_ref, wuk_ref, wuv_ref,
                   rq_ref, rk_ref, rv_ref, q_ref, ckv_ref, kpe_ref, ks_ref, *kv_refs, offs):
    o_rv, o_rq, o_rk, o_cq, o_ckv, o_kpe, n_in = offs
    x = x_ref[...]
    h = _modulated_norm(x, g_ref[...], sh_ref[...], sc_ref[...]).astype(BF16)
    p = _dot(h, w_ref[...])

    rv_ref[...] = p[:, o_rv:o_rq].astype(BF16)
    cosr, sinr = cosr_ref[...], sinr_ref[...]
    for hh in range(RET_HEADS):
        lo, hi = hh * RET_DK, (hh + 1) * RET_DK
        tq = p[:, o_rq + lo:o_rq + hi]
        rq_ref[:, lo:hi] = (tq * cosr + pltpu.roll(tq, RET_DK // 2, axis=1) * sinr).astype(BF16)
        tk = p[:, o_rk + lo:o_rk + hi]
        rk_ref[:, lo:hi] = ((tk * cosr + pltpu.roll(tk, RET_DK // 2, axis=1) * sinr) * (RET_DK ** -0.5)).astype(BF16)

    rc, rsm, rsp = rc_ref[...], rsm_ref[...], rsp_ref[...]

    cq = p[:, o_cq:o_ckv]
    cq = (cq * lax.rsqrt(jnp.mean(cq * cq, axis=-1, keepdims=True) + EPS) * gqa_ref[...]).astype(BF16)
    qf = _dot(cq, wuq_ref[...])
    aq = aq_ref[...]
    for hh in range(MLA_HEADS):
        lo, hi = hh * HEAD_SLOT, (hh + 1) * HEAD_SLOT
        blk = qf[:, lo:hi]
        qn = blk * lax.rsqrt(jnp.sum(blk * blk, axis=-1, keepdims=True) * (1.0 / QK_DIM) + EPS) * aq
        q_ref[:, lo:hi] = _rope_slot(qn, rc, rsm, rsp).astype(BF16)

    craw = p[:, o_ckv:o_kpe]
    ckv = craw * lax.rsqrt(jnp.mean(craw * craw, axis=-1, keepdims=True) + EPS) * gkva_ref[...]
    ckv_ref[...] = ckv
    ckv_bf = ckv.astype(BF16)
    kn = _dot(ckv_bf, wuk_ref[...])
    kblk = p[:, o_kpe:n_in]
    kpe_ss = jnp.sum(kblk * kblk, axis=-1, keepdims=True)
    kpe_rot = _rope_slot(kblk * ak_ref[...], rc, rsm, rsp)
    kpe_ref[...] = kpe_rot[:, NOPE_DIM:QK_DIM]
    lane16 = lax.broadcasted_iota(jnp.int32, (x.shape[0], MLA_HEADS), 1)
    ks_all = jnp.zeros((x.shape[0], MLA_HEADS), F32)
    for hh in range(MLA_HEADS):
        lo, hi = hh * HEAD_SLOT, (hh + 1) * HEAD_SLOT
        knb = kn[:, lo:hi]
        ms = (jnp.sum(knb * knb, axis=-1, keepdims=True) + kpe_ss) * (1.0 / QK_DIM)
        ksh = lax.rsqrt(ms + EPS)
        ks_all = jnp.where(lane16 == hh, ksh, ks_all)
        if kv_refs:
            kv_refs[0][:, lo:hi] = ((knb + kpe_rot) * ksh).astype(BF16)
    ks_ref[...] = ks_all
    if kv_refs:
        kv_refs[1][...] = _dot(ckv_bf, wuv_ref[...]).astype(BF16)


def _inproj_call(x3, mod3, norm_g, w_in_a, offs, tabs, gains, wuq, wuk, wuv, tl, emit_kv):
    bx, lx, d = x3.shape
    lm = mod3.shape[1]
    lm_blk = 1 if lm == 1 else tl
    cosr, sinr, rc, rsm, rsp = tabs
    gqa, gkva, aq, ak = gains
    hs = MLA_HEADS * HEAD_SLOT

    def tok(n):
        return pl.BlockSpec((None, tl, n), lambda b, i: (b, i, 0))

    def tab():
        return pl.BlockSpec((tl, LANES), lambda b, i: (i, 0))

    def out(n, dt):
        return jax.ShapeDtypeStruct((bx, lx, n), dt)

    out_specs = [tok(RET_HEADS * RET_DK), tok(RET_HEADS * RET_DK), tok(RET_HEADS * RET_DV), tok(hs),
                 tok(wuk.shape[0]), tok(ROPE_DIM), tok(MLA_HEADS)]
    out_shape = [out(RET_HEADS * RET_DK, BF16), out(RET_HEADS * RET_DK, BF16), out(RET_HEADS * RET_DV, BF16),
                 out(hs, BF16), out(wuk.shape[0], F32), out(ROPE_DIM, F32), out(MLA_HEADS, F32)]
    if emit_kv:
        out_specs += [tok(hs), tok(MLA_HEADS * V_DIM)]
        out_shape += [out(hs, BF16), out(MLA_HEADS * V_DIM, BF16)]
    return pl.pallas_call(
        functools.partial(_inproj_kernel, offs=offs),
        grid=(bx, lx // tl),
        in_specs=[
            tok(d), _mod_spec(lm_blk, d, 3), _mod_spec(lm_blk, d, 4), _const_spec((1, d)),
            _const_spec(w_in_a.shape), tab(), tab(), tab(), tab(), tab(),
            _const_spec(gqa.shape), _const_spec(gkva.shape), _const_spec(aq.shape), _const_spec(ak.shape),
            _const_spec(wuq.shape), _const_spec(wuk.shape), _const_spec(wuv.shape),
        ],
        out_specs=out_specs,
        out_shape=out_shape,
        compiler_params=_cparams("arbitrary", "arbitrary"),
        name="mixer_inproj",
    )(x3, mod3, mod3, norm_g.reshape(1, d), w_in_a, cosr, sinr, rc, rsm, rsp, gqa, gkva, aq, ak, wuq, wuk, wuv)


def _group_norm(o, g):
    mu = jnp.mean(o, axis=-1, keepdims=True)
    oc = o - mu
    var = jnp.mean(oc * oc, axis=-1, keepdims=True)
    return oc * lax.rsqrt(var + EPS) * g


def _ret_prompt_kernel(q_ref, k_ref, v_ref, dec_ref, xi_ref, zeta_ref, gch_ref, gn_ref, o_ref, sout_ref, s_scr):
    c = pl.program_id(1)

    @pl.when(c == 0)
    def _():
        s_scr[...] = jnp.zeros_like(s_scr)

    for hh in range(RET_HEADS):
        q = q_ref[:, hh * RET_DK:(hh + 1) * RET_DK]
        k = k_ref[:, hh * RET_DK:(hh + 1) * RET_DK]
        v = v_ref[:, hh * RET_DV:(hh + 1) * RET_DV]
        s = s_scr[hh]
        scores = _dot_nt(q, k) * dec_ref[hh]
        inner = _dot(scores.astype(BF16), v)
        cross = _dot(q, s.astype(BF16)) * xi_ref[hh]
        kz = (k.astype(F32) * zeta_ref[hh]).astype(BF16)
        s_new = gch_ref[hh] * s + _dot_tn(kz, v)
        s_scr[hh] = s_new
        o_ref[:, hh * RET_DV:(hh + 1) * RET_DV] = _group_norm(inner + cross, gn_ref[hh]).astype(BF16)

    @pl.when(c == pl.num_programs(1) - 1)
    def _():
        sout_ref[...] = s_scr[...]


def _ret_prompt_call(rq, rk, rv, ret_tabs, gn_g):
    bx, lx, _ = rq.shape
    ch = RET_CHUNK
    dec, xi, zeta, gch = ret_tabs
    return pl.pallas_call(
        _ret_prompt_kernel,
        grid=(bx, lx // ch),
        in_specs=[
            pl.BlockSpec((None, ch, RET_HEADS * RET_DK), lambda b, c: (b, c, 0)),
            pl.BlockSpec((None, ch, RET_HEADS * RET_DK), lambda b, c: (b, c, 0)),
            pl.BlockSpec((None, ch, RET_HEADS * RET_DV), lambda b, c: (b, c, 0)),
            _const_spec(dec.shape), _const_spec(xi.shape), _const_spec(zeta.shape), _const_spec(gch.shape),
            _const_spec((RET_HEADS, 1, RET_DV)),
        ],
        out_specs=[
            pl.BlockSpec((None, ch, RET_HEADS * RET_DV), lambda b, c: (b, c, 0)),
            pl.BlockSpec((None, RET_HEADS, RET_DK, RET_DV), lambda b, c: (b, 0, 0, 0)),
        ],
        out_shape=[
            jax.ShapeDtypeStruct((bx, lx, RET_HEADS * RET_DV), BF16),
            jax.ShapeDtypeStruct((bx, RET_HEADS, RET_DK, RET_DV), F32),
        ],
        scratch_shapes=[pltpu.VMEM((RET_HEADS, RET_DK, RET_DV), F32)],
        compiler_params=_cparams("arbitrary", "arbitrary"),
        name="retention_prompt",
    )(rq, rk, rv, dec, xi, zeta, gch, gn_g.reshape(RET_HEADS, 1, RET_DV))


def _ret_sample_kernel(qt_ref, kt_ref, v_ref, s_ref, dec_ref, xi_ref, zeta_ref, gch_ref, gn_ref, o_ref, sout_ref, *, bb):
    for j in range(bb):
        for hh in range(RET_HEADS):
            qcol = qt_ref[hh * RET_DK:(hh + 1) * RET_DK, j:j + 1].astype(F32)
            kcol = kt_ref[hh * RET_DK:(hh + 1) * RET_DK, j:j + 1].astype(F32)
            vrow = v_ref[j:j + 1, hh * RET_DV:(hh + 1) * RET_DV].astype(F32)
            s0 = s_ref[j, hh]
            score = jnp.sum(qcol * kcol, axis=0, keepdims=True) * dec_ref[hh]
            inner = score * vrow
            cross = jnp.sum(qcol * s0, axis=0, keepdims=True) * xi_ref[hh]
            sout_ref[j, hh] = gch_ref[hh] * s0 + (kcol * zeta_ref[hh]) * vrow
            o_ref[j:j + 1, hh * RET_DV:(hh + 1) * RET_DV] = _group_norm(inner + cross, gn_ref[hh]).astype(BF16)


def _ret_sample_call(rq, rk, rv, state, ret_tabs, gn_g, bb):
    n = rq.shape[0]
    dec, xi, zeta, gch = ret_tabs
    dqk = RET_HEADS * RET_DK
    qt = rq.reshape(n // bb, bb, dqk).transpose(0, 2, 1)
    kt = rk.reshape(n // bb, bb, dqk).transpose(0, 2, 1)
    return pl.pallas_call(
        functools.partial(_ret_sample_kernel, bb=bb),
        grid=(n // bb,),
        in_specs=[
            pl.BlockSpec((None, dqk, bb), lambda i: (i, 0, 0)),
            pl.BlockSpec((None, dqk, bb), lambda i: (i, 0, 0)),
            pl.BlockSpec((bb, RET_HEADS * RET_DV), lambda i: (i, 0)),
            pl.BlockSpec((bb, RET_HEADS, RET_DK, RET_DV), lambda i: (i, 0, 0, 0)),
            _const_spec(dec.shape), _const_spec(xi.shape), _const_spec(zeta.shape), _const_spec(gch.shape),
            _const_spec((RET_HEADS, 1, RET_DV)),
        ],
        out_specs=[
            pl.BlockSpec((bb, RET_HEADS * RET_DV), lambda i: (i, 0)),
            pl.BlockSpec((bb, RET_HEADS, RET_DK, RET_DV), lambda i: (i, 0, 0, 0)),
        ],
        out_shape=[
            jax.ShapeDtypeStruct((n, RET_HEADS * RET_DV), BF16),
            jax.ShapeDtypeStruct(state.shape, F32),
        ],
        compiler_params=_cparams("arbitrary"),
        name="retention_sample",
    )(qt, kt, rv, state, dec, xi, zeta, gch, gn_g.reshape(RET_HEADS, 1, RET_DV))


HEADS_PER_GROUP = 4


def _attn_prompt_kernel(q_ref, k_ref, v_ref, o_ref, m_scr, l_scr, acc_scr, *, tq):
    i = pl.program_id(2)
    m_scr[...] = jnp.full_like(m_scr, NEG_INF)
    l_scr[...] = jnp.zeros_like(l_scr)
    acc_scr[...] = jnp.zeros_like(acc_scr)

    def tile(kt, masked):
        ks = pl.multiple_of(kt * tq, tq)
        vblk = v_ref[pl.ds(ks, tq), :]
        for hh in range(HEADS_PER_GROUP):
            q = q_ref[:, hh * HEAD_SLOT:(hh + 1) * HEAD_SLOT]
            k = k_ref[pl.ds(ks, tq), hh * HEAD_SLOT:(hh + 1) * HEAD_SLOT]
            s = _dot_nt(q, k)
            if masked:
                row = lax.broadcasted_iota(jnp.int32, s.shape, 0)
                col = lax.broadcasted_iota(jnp.int32, s.shape, 1)
                s = jnp.where(row >= col, s, NEG_INF)
            m_old = m_scr[hh]
            m_new = jnp.maximum(m_old, jnp.max(s, axis=-1, keepdims=True))
            alpha = jnp.exp(m_old - m_new)
            p = jnp.exp(s - m_new)
            l_scr[hh] = alpha * l_scr[hh] + jnp.sum(p, axis=-1, keepdims=True)
            acc_scr[hh] = alpha * acc_scr[hh] + _dot(p.astype(BF16), vblk)
            m_scr[hh] = m_new

    def body(kt, carry):
        tile(kt, False)
        return carry

    lax.fori_loop(0, i, body, 0)
    tile(i, True)

    lane = lax.broadcasted_iota(jnp.int32, o_ref.shape, 1)
    o = jnp.zeros(o_ref.shape, F32)
    for hh in range(HEADS_PER_GROUP):
        oh = acc_scr[hh] / l_scr[hh]
        o = jnp.where((lane >= hh * V_DIM) & (lane < (hh + 1) * V_DIM), oh, o)
    o_ref[...] = o.astype(BF16)


def _attn_prompt_call(q, k, v, tq):
    bx, lx, _ = q.shape
    gw = HEADS_PER_GROUP * HEAD_SLOT
    vw = HEADS_PER_GROUP * V_DIM
    ng = MLA_HEADS // HEADS_PER_GROUP
    return pl.pallas_call(
        functools.partial(_attn_prompt_kernel, tq=tq),
        grid=(bx, ng, lx // tq),
        in_specs=[
            pl.BlockSpec((None, tq, gw), lambda b, g, i: (b, i, g)),
            pl.BlockSpec((None, lx, gw), lambda b, g, i: (b, 0, g)),
            pl.BlockSpec((None, lx, vw), lambda b, g, i: (b, 0, g)),
        ],
        out_specs=pl.BlockSpec((None, tq, vw), lambda b, g, i: (b, i, g)),
        out_shape=jax.ShapeDtypeStruct((bx, lx, MLA_HEADS * V_DIM), BF16),
        scratch_shapes=[
            pltpu.VMEM((HEADS_PER_GROUP, tq, 1), F32),
            pltpu.VMEM((HEADS_PER_GROUP, tq, 1), F32),
            pltpu.VMEM((HEADS_PER_GROUP, tq, vw), F32),
        ],
        compiler_params=_cparams("arbitrary", "arbitrary", "arbitrary"),
        name="attention_prompt",
    )(q, k, v)


def _qlat_kernel(q_ref, w_ref, o_ref):
    o_ref[...] = _dot(q_ref[...], w_ref[...]).astype(BF16)


def _qlat_call(q2, wukt):
    n = q2.shape[0]
    r = wukt.shape[2]
    return pl.pallas_call(
        _qlat_kernel,
        grid=(MLA_HEADS,),
        in_specs=[
            pl.BlockSpec((n, HEAD_SLOT), lambda h: (0, h)),
            pl.BlockSpec((None, HEAD_SLOT, r), lambda h: (h, 0, 0)),
        ],
        out_specs=pl.BlockSpec((n, r), lambda h: (0, h)),
        out_shape=jax.ShapeDtypeStruct((n, MLA_HEADS * r), BF16),
        compiler_params=_cparams("arbitrary"),
        name="absorb_q",
    )(q2, wukt)


NEW_ROWS = 8


def _attn_paged_kernel(pt_ref, qlat_ref, q_ref, cnew_ref, pnew_ref, snew_ref, eye_ref,
                       ckv_hbm, kpe_hbm, ks_hbm, o_ref,
                       ckv_buf, kpe_buf, ks_buf, sem, *, layer, gp, n_chunk):
    b = pl.program_id(0)
    nb = pl.num_programs(0)

    def copies(bi, ci, slot):
        out = []
        for g in range(gp):
            page = pt_ref[bi, ci * gp + g]
            out.append(pltpu.make_async_copy(ckv_hbm.at[layer, page], ckv_buf.at[slot, g], sem.at[slot, 0]))
            out.append(pltpu.make_async_copy(kpe_hbm.at[layer, page], kpe_buf.at[slot, g], sem.at[slot, 1]))
            out.append(pltpu.make_async_copy(ks_hbm.at[layer, page], ks_buf.at[slot, g], sem.at[slot, 2]))
        return out

    def start(bi, ci, slot):
        for cp in copies(bi, ci, slot):
            cp.start()

    @pl.when(b == 0)
    def _():
        start(0, 0, 0)

    qlat = qlat_ref[...]
    qpe = q_ref[:, NOPE_DIM:QK_DIM]
    eye = eye_ref[...]

    def scores(ckv_bf, kpe_f32, ks_f32):
        s = _dot_nt(qlat, ckv_bf) + _dot_nt(qpe, kpe_f32.astype(BF16))
        hi = ks_f32.astype(BF16)
        lo = (ks_f32 - hi.astype(F32)).astype(BF16)
        return s * (_dot_nt(eye, hi) + _dot_nt(eye, lo))

    def merge(carry, s, vals_bf):
        m_old, l_old, acc = carry
        m_new = jnp.maximum(m_old, jnp.max(s, axis=-1, keepdims=True))
        alpha = jnp.exp(m_old - m_new)
        p = jnp.exp(s - m_new)
        l_new = alpha * l_old + jnp.sum(p, axis=-1, keepdims=True)
        return m_new, l_new, alpha * acc + _dot(p.astype(BF16), vals_bf)

    def body(ci, carry):
        slot = ci % 2

        @pl.when(ci + 1 < n_chunk)
        def _():
            start(b, ci + 1, 1 - slot)

        @pl.when((ci + 1 == n_chunk) & (b + 1 < nb))
        def _():
            start(b + 1, 0, 1 - slot)

        for cp in copies(b, ci, slot):
            cp.wait()
        keys = gp * PAGE_SIZE
        ckv_bf = ckv_buf[slot].reshape(keys, ckv_buf.shape[-1]).astype(BF16)
        kpe = kpe_buf[slot].reshape(keys, kpe_buf.shape[-1])
        ks = ks_buf[slot].reshape(keys, ks_buf.shape[-1])
        return merge(carry, scores(ckv_bf, kpe, ks), ckv_bf)

    h = qlat.shape[0]
    init = (jnp.full((h, 1), NEG_INF, F32), jnp.zeros((h, 1), F32), jnp.zeros((h, qlat.shape[1]), F32))
    carry = lax.fori_loop(0, n_chunk, body, init)

    cnew_bf = cnew_ref[...].astype(BF16)
    s_new = scores(cnew_bf, pnew_ref[...], snew_ref[...])
    col = lax.broadcasted_iota(jnp.int32, s_new.shape, 1)
    s_new = jnp.where(col == 0, s_new, NEG_INF)
    _, l_fin, acc = merge(carry, s_new, cnew_bf)
    o_ref[...] = (acc / l_fin).astype(BF16)


def _attn_paged_call(page_table, qlat3, q3, cnew, pnew, snew, cache_ckv, cache_kpe, cache_ks, layer, gp):
    nb, n_pages = page_table.shape
    assert n_pages % (2 * gp) == 0, "chunk count must be even so buffer slots alternate across sequences"
    n_chunk = n_pages // gp
    r = cache_ckv.shape[-1]
    eye = jnp.eye(MLA_HEADS, dtype=BF16)
    grid_spec = pltpu.PrefetchScalarGridSpec(
        num_scalar_prefetch=1,
        grid=(nb,),
        in_specs=[
            pl.BlockSpec((None, MLA_HEADS, r), lambda b, pt: (b, 0, 0)),
            pl.BlockSpec((None, MLA_HEADS, HEAD_SLOT), lambda b, pt: (b, 0, 0)),
            pl.BlockSpec((None, NEW_ROWS, r), lambda b, pt: (b, 0, 0)),
            pl.BlockSpec((None, NEW_ROWS, ROPE_DIM), lambda b, pt: (b, 0, 0)),
            pl.BlockSpec((None, NEW_ROWS, MLA_HEADS), lambda b, pt: (b, 0, 0)),
            pl.BlockSpec((MLA_HEADS, MLA_HEADS), lambda b, pt: (0, 0)),
            pl.BlockSpec(memory_space=pl.ANY),
            pl.BlockSpec(memory_space=pl.ANY),
            pl.BlockSpec(memory_space=pl.ANY),
        ],
        out_specs=pl.BlockSpec((None, MLA_HEADS, r), lambda b, pt: (b, 0, 0)),
        scratch_shapes=[
            pltpu.VMEM((2, gp, PAGE_SIZE, r), F32),
            pltpu.VMEM((2, gp, PAGE_SIZE, ROPE_DIM), F32),
            pltpu.VMEM((2, gp, PAGE_SIZE, MLA_HEADS), F32),
            pltpu.SemaphoreType.DMA((2, 3)),
        ],
    )
    return pl.pallas_call(
        functools.partial(_attn_paged_kernel, layer=layer, gp=gp, n_chunk=n_chunk),
        grid_spec=grid_spec,
        out_shape=jax.ShapeDtypeStruct((nb, MLA_HEADS, r), BF16),
        compiler_params=_cparams("arbitrary"),
        name="attention_paged",
    )(page_table, qlat3, q3, cnew, pnew, snew, eye, cache_ckv, cache_kpe, cache_ks)


def _uv_pair_kernel(o_ref, w_ref, y_ref):
    r = w_ref.shape[1]
    parts = [_dot(o_ref[:, j * r:(j + 1) * r], w_ref[j]) for j in range(w_ref.shape[0])]
    y_ref[...] = jnp.concatenate(parts, axis=-1).astype(BF16)


def _uv_call(olat2, wuv_h):
    n = olat2.shape[0]
    r = wuv_h.shape[1]
    hpb = LANES // V_DIM
    return pl.pallas_call(
        _uv_pair_kernel,
        grid=(MLA_HEADS // hpb,),
        in_specs=[
            pl.BlockSpec((n, hpb * r), lambda h: (0, h)),
            pl.BlockSpec((hpb, r, V_DIM), lambda h: (h, 0, 0)),
        ],
        out_specs=pl.BlockSpec((n, hpb * V_DIM), lambda h: (0, h)),
        out_shape=jax.ShapeDtypeStruct((n, MLA_HEADS * V_DIM), BF16),
        compiler_params=_cparams("arbitrary"),
        name="value_up",
    )(olat2, wuv_h)


def _outproj_kernel(x_ref, sh_ref, sc_ref, gt_ref, g_ref, wg_ref, ret_ref, att_ref, wpa_ref, wpb_ref, wo_ref, o_ref):
    x = x_ref[...]
    d = x.shape[-1]
    h = _modulated_norm(x, g_ref[...], sh_ref[...], sc_ref[...]).astype(BF16)
    gates = _dot(h, wg_ref[...])
    nv = ret_ref.shape[-1]
    rg, ga, gb = gates[:, :nv], gates[:, nv:nv + d], gates[:, nv + d:]
    ya = _dot((ret_ref[...].astype(F32) * (rg * jax.nn.sigmoid(rg))).astype(BF16), wpa_ref[...])
    yb = _dot(att_ref[...], wpb_ref[...])
    merged = jax.nn.sigmoid(ga) * ya + jax.nn.sigmoid(gb) * yb
    o_ref[...] = x + gt_ref[...] * _dot(merged.astype(BF16), wo_ref[...])


def _outproj_call(x3, mod3, norm_g, w_gates, ret_o, att_o, wpa, wpb, wo, tl):
    bx, lx, d = x3.shape
    lm = mod3.shape[1]
    lm_blk = 1 if lm == 1 else tl

    def tok(n):
        return pl.BlockSpec((None, tl, n), lambda b, i: (b, i, 0))

    return pl.pallas_call(
        _outproj_kernel,
        grid=(bx, lx // tl),
        in_specs=[
            tok(d), _mod_spec(lm_blk, d, 3), _mod_spec(lm_blk, d, 4), _mod_spec(lm_blk, d, 5), _const_spec((1, d)),
            _const_spec(w_gates.shape), tok(ret_o.shape[-1]), tok(att_o.shape[-1]),
            _const_spec(wpa.shape), _const_spec(wpb.shape), _const_spec(wo.shape),
        ],
        out_specs=tok(d),
        out_shape=jax.ShapeDtypeStruct(x3.shape, F32),
        compiler_params=_cparams("arbitrary", "arbitrary"),
        name="mixer_outproj",
    )(x3, mod3, mod3, mod3, norm_g.reshape(1, d), w_gates, ret_o, att_o, wpa, wpb, wo)


def _rope_tables(pos):
    half_r = RET_DK // 2
    fr = ROPE_THETA ** (-jnp.arange(half_r, dtype=F32) / half_r)
    ang = pos[:, None] * fr[None, :]
    cos, sin = jnp.cos(ang), jnp.sin(ang)
    cosr = jnp.concatenate([cos, cos], axis=-1)
    sinr = jnp.concatenate([-sin, sin], axis=-1)
    half_m = ROPE_DIM // 2
    fm = ROPE_THETA ** (-jnp.arange(half_m, dtype=F32) / half_m)
    angm = pos[:, None] * fm[None, :]
    cm, sm = jnp.cos(angm), jnp.sin(angm)
    n = pos.shape[0]
    z16, z32 = jnp.zeros((n, half_m), F32), jnp.zeros((n, LANES - QK_DIM), F32)
    rc = jnp.concatenate([jnp.ones((n, NOPE_DIM), F32), cm, cm, z32], axis=-1)
    rsm = jnp.concatenate([jnp.zeros((n, NOPE_DIM), F32), -sm, z16, z32], axis=-1)
    rsp = jnp.concatenate([jnp.zeros((n, NOPE_DIM), F32), z16, sm, z32], axis=-1)
    return cosr, sinr, rc, rsm, rsp


def _retention_tables(chunk):
    log_gamma = jnp.log1p(-jnp.exp2(-5.0 - jnp.arange(RET_HEADS, dtype=F32)))
    idx = jnp.arange(chunk, dtype=F32)
    rel = idx[:, None] - idx[None, :]
    decay = jnp.where(rel[None] >= 0, jnp.exp(rel[None] * log_gamma[:, None, None]), 0.0)
    xi = jnp.exp((idx[None, :] + 1.0) * log_gamma[:, None])[:, :, None]
    zeta = jnp.exp((chunk - 1.0 - idx)[None, :] * log_gamma[:, None])[:, :, None]
    gch = jnp.exp(chunk * log_gamma)[:, None, None]
    return decay, xi, zeta, gch


def _pad_heads(w, width):
    r, hn, n = w.shape
    return jnp.pad(w, ((0, 0), (0, 0), (0, width - n))).reshape(r, hn * width)


def _prep_layer(l, w_in, ffn_w13, ffn_w2, w_uq, w_uk, w_uv, w_pa, w_pb, w_out, mla_qa_g, mla_kva_g, mla_qn_g, mla_kn_g):
    d = w_in.shape[1]
    rqk, rvw = RET_HEADS * RET_DK, RET_HEADS * RET_DV
    q_rank, kv_rank = w_uq.shape[1], w_uk.shape[1]
    o = np.cumsum([0, rqk, rqk, rvw, rvw, q_rank, kv_rank, ROPE_DIM, d, d])
    wi = w_in[l]
    rq, rk, rv, rg, cq, ckv, kpe, ga, gb = [wi[:, o[i]:o[i + 1]] for i in range(9)]
    kpe_slot = jnp.pad(kpe, ((0, 0), (NOPE_DIM, LANES - QK_DIM)))
    w_a = jnp.concatenate([rv, rq, rk, cq, ckv, kpe_slot], axis=1).astype(BF16)
    offs = tuple(int(v) for v in np.cumsum([0, rvw, rqk, rqk, q_rank, kv_rank, LANES]))
    w_g = jnp.concatenate([rg, ga, gb], axis=1).astype(BF16)
    wuq = _pad_heads(w_uq[l], HEAD_SLOT).astype(BF16)
    wuk = _pad_heads(w_uk[l], HEAD_SLOT).astype(BF16)
    wuv = w_uv[l].reshape(kv_rank, MLA_HEADS * V_DIM).astype(BF16)
    wukt = jnp.pad(jnp.transpose(w_uk[l], (1, 2, 0)), ((0, 0), (0, HEAD_SLOT - NOPE_DIM), (0, 0))).astype(BF16)
    wuv_h = jnp.transpose(w_uv[l], (1, 0, 2)).astype(BF16)
    gqn, gkn = mla_qn_g[l], mla_kn_g[l]
    zpad = jnp.zeros((LANES - QK_DIM,), F32)
    aq = (jnp.concatenate([gqn[:NOPE_DIM] * gkn[:NOPE_DIM], gqn[NOPE_DIM:], zpad]) * MLA_SCALE).reshape(1, LANES)
    ak = jnp.concatenate([jnp.zeros((NOPE_DIM,), F32), gkn[NOPE_DIM:], zpad]).reshape(1, LANES)
    gains = (mla_qa_g[l].reshape(1, q_rank), mla_kva_g[l].reshape(1, kv_rank), aq, ak)
    return dict(w_a=w_a, offs=offs, w_g=w_g, wuq=wuq, wuk=wuk, wuv=wuv, wukt=wukt, wuv_h=wuv_h, gains=gains,
                w13=[ffn_w13[l, j].astype(BF16) for j in range(2)], w2=[ffn_w2[l, j].astype(BF16) for j in range(2)],
                wpa=w_pa[l].astype(BF16), wpb=w_pb[l].astype(BF16), wo=w_out[l].astype(BF16))


def _largest_tile(n, cap):
    t = min(n, cap)
    while n % t:
        t //= 2
    return t


def kernel(x_prompt, x_sample, cache_ckv, cache_kpe, cache_kscale, state_ret, page_table, c_prompt, c_sample, ada_w, ada_b, norm_g, ffn_w13, ffn_w2, w_in, ret_gn_g, mla_qa_g, mla_kva_g, w_uq, mla_qn_g, mla_kn_g, w_uk, w_uv, w_pa, w_pb, w_out):
    depth = ada_w.shape[0]
    bp, lp, d = x_prompt.shape
    ns, ls, _ = x_sample.shape
    assert ls == 1, "sample group is one new token per sequence"
    assert lp % RET_CHUNK == 0
    n_pages = page_table.shape[1]
    past_len = n_pages * PAGE_SIZE

    mod = _mod_call(jnp.concatenate([c_prompt, c_sample], axis=0), ada_w.astype(BF16), ada_b)
    tabs_p = _rope_tables(jnp.arange(lp, dtype=F32))
    tabs_s = _rope_tables(jnp.full((ns,), float(past_len), F32))
    rt_p = _retention_tables(RET_CHUNK)
    rt_s = _retention_tables(1)

    tl_p = _largest_tile(lp, 512)
    tl_in = _largest_tile(lp, 256)
    tq = _largest_tile(lp, 512)
    bb = _largest_tile(ns, 8)
    gp = _largest_tile(n_pages // 2, 8)

    yp = x_prompt
    ys = x_sample.reshape(1, ns, d)
    st_p, st_s = [], []
    for l in range(depth):
        w = _prep_layer(l, w_in, ffn_w13, ffn_w2, w_uq, w_uk, w_uv, w_pa, w_pb, w_out,
                        mla_qa_g, mla_kva_g, mla_qn_g, mla_kn_g)
        mod_p = mod[l, :bp].reshape(bp, 1, N_MOD * d)
        mod_s = mod[l, bp:].reshape(1, ns, N_MOD * d)
        ng = norm_g[l]

        yp = _ffn_call(yp, mod_p, 0, ng[0], w["w13"][0], w["w2"][0], tl_p)
        rq, rk, rv, q, ckv, kpe, ks, kf, vf = _inproj_call(
            yp, mod_p, ng[1], w["w_a"], w["offs"], tabs_p, w["gains"], w["wuq"], w["wuk"], w["wuv"], tl_in, True)
        ret_o, ret_s = _ret_prompt_call(rq, rk, rv, rt_p, ret_gn_g[l])
        att_o = _attn_prompt_call(q, kf, vf, tq)
        yp = _outproj_call(yp, mod_p, ng[1], w["w_g"], ret_o, att_o, w["wpa"], w["wpb"], w["wo"], tl_p)
        yp = _ffn_call(yp, mod_p, 6, ng[2], w["w13"][1], w["w2"][1], tl_p)
        st_p.append((ckv, kpe, ks, ret_s))

        ys = _ffn_call(ys, mod_s, 0, ng[0], w["w13"][0], w["w2"][0], ns)
        rq, rk, rv, q, ckv, kpe, ks = _inproj_call(
            ys, mod_s, ng[1], w["w_a"], w["offs"], tabs_s, w["gains"], w["wuq"], w["wuk"], w["wuv"], ns, False)
        ret_o, ret_s = _ret_sample_call(rq[0], rk[0], rv[0], state_ret[l], rt_s, ret_gn_g[l], bb)
        qlat = _qlat_call(q[0], w["wukt"])

        def new_rows(a):
            return jnp.pad(a[0][:, None, :], ((0, 0), (0, NEW_ROWS - 1), (0, 0)))

        olat = _attn_paged_call(page_table, qlat.reshape(ns, MLA_HEADS, -1), q[0].reshape(ns, MLA_HEADS, HEAD_SLOT),
                                new_rows(ckv), new_rows(kpe), new_rows(ks),
                                cache_ckv, cache_kpe, cache_kscale, l, gp)
        att_o = _uv_call(olat.reshape(ns, -1), w["wuv_h"])
        ys = _outproj_call(ys, mod_s, ng[1], w["w_g"], ret_o[None], att_o[None], w["wpa"], w["wpb"], w["wo"], ns)
        ys = _ffn_call(ys, mod_s, 6, ng[2], w["w13"][1], w["w2"][1], ns)
        st_s.append((ckv.reshape(ns, 1, -1), kpe.reshape(ns, 1, -1), ks.reshape(ns, 1, -1), ret_s))

    def stack(states, i):
        return jnp.stack([st[i] for st in states], axis=0)

    return (yp, ys.reshape(ns, 1, d),
            stack(st_p, 0), stack(st_p, 1), stack(st_p, 2), stack(st_p, 3),
            stack(st_s, 0), stack(st_s, 1), stack(st_s, 2), stack(st_s, 3))
```

```python
import functools

import jax
import jax.numpy as jnp
import numpy as np
from jax import lax
from jax.experimental import pallas as pl
from jax.experimental.pallas import tpu as pltpu

F32 = jnp.float32
BF16 = jnp.bfloat16

RET_HEADS = 4
RET_DK = 128
RET_DV = 256
RET_CHUNK = 128
MLA_HEADS = 16
NOPE_DIM = 64
ROPE_DIM = 32
V_DIM = 64
QK_DIM = NOPE_DIM + ROPE_DIM
MLA_SCALE = QK_DIM ** -0.5
LOG2E = float(np.log2(np.e))
ROPE_THETA = 10000.0
N_MOD = 9
EPS = 1e-6
PAGE_SIZE = 128

LANES = 128
HEAD_SLOT = LANES
VMEM_LIMIT = 56 * 1024 * 1024
NEG_INF = float("-inf")


def _cparams(*sem):
    return pltpu.CompilerParams(dimension_semantics=sem, vmem_limit_bytes=VMEM_LIMIT)


def _dot(a, b):
    return jnp.dot(a, b, preferred_element_type=F32)


def _dot_nt(a, b):
    return lax.dot_general(a, b, (((1,), (1,)), ((), ())), preferred_element_type=F32)


def _dot_tn(a, b):
    return lax.dot_general(a, b, (((0,), (0,)), ((), ())), preferred_element_type=F32)


def _const_spec(shape):
    n = len(shape)
    return pl.BlockSpec(shape, lambda *_: (0,) * n)


def _modulated_norm(x, g, shift, scale):
    ms = jnp.mean(x * x, axis=-1, keepdims=True)
    y = x * lax.rsqrt(ms + EPS) * g
    return y * (1.0 + scale) + shift


def _mod_spec(lm_blk, d, k):
    if lm_blk == 1:
        return pl.BlockSpec((None, 1, d), lambda b, i: (b, 0, k))
    return pl.BlockSpec((None, lm_blk, d), lambda b, i: (b, i, k))


def _mod_kernel(c_ref, w_ref, b_ref, o_ref):
    c = c_ref[...]
    s = (c * jax.nn.sigmoid(c)).astype(BF16)
    o_ref[...] = _dot(s, w_ref[...]) + b_ref[...]


def _mod_call(c_all, ada_w_bf, ada_b):
    depth, d, nd = ada_w_bf.shape
    n = c_all.shape[0]
    return pl.pallas_call(
        _mod_kernel,
        grid=(depth, nd // d),
        in_specs=[
            pl.BlockSpec((n, d), lambda l, j: (0, 0)),
            pl.BlockSpec((None, d, d), lambda l, j: (l, 0, j)),
            pl.BlockSpec((None, 1, d), lambda l, j: (l, 0, j)),
        ],
        out_specs=pl.BlockSpec((None, n, d), lambda l, j: (l, 0, j)),
        out_shape=jax.ShapeDtypeStruct((depth, n, nd), F32),
        compiler_params=_cparams("arbitrary", "arbitrary"),
        name="adaln_mod",
    )(c_all, ada_w_bf, ada_b.reshape(depth, 1, nd))


def _ffn_kernel(x_ref, sh_ref, sc_ref, gt_ref, g_ref, w13_ref, w2_ref, o_ref, *, d_ff):
    x = x_ref[...]
    h = _modulated_norm(x, g_ref[...], sh_ref[...], sc_ref[...]).astype(BF16)
    ab = _dot(h, w13_ref[...])
    a, b = ab[:, :d_ff], ab[:, d_ff:]
    u = (a * jax.nn.sigmoid(a) * b).astype(BF16)
    y = _dot(u, w2_ref[...])
    o_ref[...] = x + 0.5 * gt_ref[...] * y


def _ffn_call(x3, mod3, k0, norm_g, w13, w2, tl):
    bx, lx, d = x3.shape
    lm = mod3.shape[1]
    lm_blk = 1 if lm == 1 else tl
    d_ff = w2.shape[0]
    return pl.pallas_call(
        functools.partial(_ffn_kernel, d_ff=d_ff),
        grid=(bx, lx // tl),
        in_specs=[
            pl.BlockSpec((None, tl, d), lambda b, i: (b, i, 0)),
            _mod_spec(lm_blk, d, k0), _mod_spec(lm_blk, d, k0 + 1), _mod_spec(lm_blk, d, k0 + 2),
            _const_spec((1, d)),
            pl.BlockSpec(w13.shape, lambda b, i: (0, 0), pipeline_mode=pl.Buffered(1)),
            pl.BlockSpec(w2.shape, lambda b, i: (0, 0), pipeline_mode=pl.Buffered(1)),
        ],
        out_specs=pl.BlockSpec((None, tl, d), lambda b, i: (b, i, 0)),
        out_shape=jax.ShapeDtypeStruct(x3.shape, F32),
        compiler_params=_cparams("arbitrary", "arbitrary"),
        name="ffn",
    )(x3, mod3, mod3, mod3, norm_g.reshape(1, d), w13, w2)


def _rope_slot(t, c, sm, sp):
    return t * c + pltpu.roll(t, LANES - ROPE_DIM // 2, axis=1) * sm + pltpu.roll(t, ROPE_DIM // 2, axis=1) * sp


def _inproj_kernel(x_ref, sh_ref, sc_ref, g_ref, w_ref, cosr_ref, sinr_ref, rc_ref, rsm_ref, rsp_ref,
                   gqa_ref, gkva_ref, aq_ref, ak_ref, wuq_ref, wuk_ref, wuv_ref,
                   rq_ref, rk_ref, rv_ref, q_ref, ckv_ref, kpe_ref, ks_ref, *kv_refs, offs):
    o_rv, o_rq, o_rk, o_cq, o_ckv, o_kpe, n_in = offs
    x = x_ref[...]
    h = _modulated_norm(x, g_ref[...], sh_ref[...], sc_ref[...]).astype(BF16)
    p = _dot(h, w_ref[...])

    rv_ref[...] = p[:, o_rv:o_rq].astype(BF16)
    cosr, sinr = cosr_ref[...], sinr_ref[...]
    for hh in range(RET_HEADS):
        lo, hi = hh * RET_DK, (hh + 1) * RET_DK
        tq = p[:, o_rq + lo:o_rq + hi]
        rq_ref[:, lo:hi] = (tq * cosr + pltpu.roll(tq, RET_DK // 2, axis=1) * sinr).astype(BF16)
        tk = p[:, o_rk + lo:o_rk + hi]
        rk_ref[:, lo:hi] = ((tk * cosr + pltpu.roll(tk, RET_DK // 2, axis=1) * sinr) * (RET_DK ** -0.5)).astype(BF16)

    rc, rsm, rsp = rc_ref[...], rsm_ref[...], rsp_ref[...]

    cq = p[:, o_cq:o_ckv]
    cq = (cq * lax.rsqrt(jnp.mean(cq * cq, axis=-1, keepdims=True) + EPS) * gqa_ref[...]).astype(BF16)
    qf = _dot(cq, wuq_ref[...])
    aq = aq_ref[...]
    for hh in range(MLA_HEADS):
        lo, hi = hh * HEAD_SLOT, (hh + 1) * HEAD_SLOT
        blk = qf[:, lo:hi]
        qn = blk * lax.rsqrt(jnp.sum(blk * blk, axis=-1, keepdims=True) * (1.0 / QK_DIM) + EPS) * aq
        q_ref[:, lo:hi] = _rope_slot(qn, rc, rsm, rsp).astype(BF16)

    craw = p[:, o_ckv:o_kpe]
    ckv = craw * lax.rsqrt(jnp.mean(craw * craw, axis=-1, keepdims=True) + EPS) * gkva_ref[...]
    ckv_ref[...] = ckv
    ckv_bf = ckv.astype(BF16)
    kn = _dot(ckv_bf, wuk_ref[...])
    kblk = p[:, o_kpe:n_in]
    kpe_ss = jnp.sum(kblk * kblk, axis=-1, keepdims=True)
    kpe_rot = _rope_slot(kblk * ak_ref[...], rc, rsm, rsp)
    kpe_ref[...] = kpe_rot[:, NOPE_DIM:QK_DIM]
    lane16 = lax.broadcasted_iota(jnp.int32, (x.shape[0], MLA_HEADS), 1)
    ks_all = jnp.zeros((x.shape[0], MLA_HEADS), F32)
    for hh in range(MLA_HEADS):
        lo, hi = hh * HEAD_SLOT, (hh + 1) * HEAD_SLOT
        knb = kn[:, lo:hi]
        ms = (jnp.sum(knb * knb, axis=-1, keepdims=True) + kpe_ss) * (1.0 / QK_DIM)
        ksh = lax.rsqrt(ms + EPS)
        ks_all = jnp.where(lane16 == hh, ksh, ks_all)
        if kv_refs:
            kv_refs[0][:, lo:hi] = ((knb + kpe_rot) * ksh).astype(BF16)
    ks_ref[...] = ks_all
    if kv_refs:
        kv_refs[1][...] = _dot(ckv_bf, wuv_ref[...]).astype(BF16)


def _inproj_call(x3, mod3, norm_g, w_in_a, offs, tabs, gains, wuq, wuk, wuv, tl, emit_kv):
    bx, lx, d = x3.shape
    lm = mod3.shape[1]
    lm_blk = 1 if lm == 1 else tl
    cosr, sinr, rc, rsm, rsp = tabs
    gqa, gkva, aq, ak = gains
    hs = MLA_HEADS * HEAD_SLOT

    def tok(n):
        return pl.BlockSpec((None, tl, n), lambda b, i: (b, i, 0))

    def tab():
        return pl.BlockSpec((tl, LANES), lambda b, i: (i, 0))

    def out(n, dt):
        return jax.ShapeDtypeStruct((bx, lx, n), dt)

    out_specs = [tok(RET_HEADS * RET_DK), tok(RET_HEADS * RET_DK), tok(RET_HEADS * RET_DV), tok(hs),
                 tok(wuk.shape[0]), tok(ROPE_DIM), tok(MLA_HEADS)]
    out_shape = [out(RET_HEADS * RET_DK, BF16), out(RET_HEADS * RET_DK, BF16), out(RET_HEADS * RET_DV, BF16),
                 out(hs, BF16), out(wuk.shape[0], F32), out(ROPE_DIM, F32), out(MLA_HEADS, F32)]
    if emit_kv:
        out_specs += [tok(hs), tok(MLA_HEADS * V_DIM)]
        out_shape += [out(hs, BF16), out(MLA_HEADS * V_DIM, BF16)]
    return pl.pallas_call(
        functools.partial(_inproj_kernel, offs=offs),
        grid=(bx, lx // tl),
        in_specs=[
            tok(d), _mod_spec(lm_blk, d, 3), _mod_spec(lm_blk, d, 4), _const_spec((1, d)),
            _const_spec(w_in_a.shape), tab(), tab(), tab(), tab(), tab(),
            _const_spec(gqa.shape), _const_spec(gkva.shape), _const_spec(aq.shape), _const_spec(ak.shape),
            _const_spec(wuq.shape), _const_spec(wuk.shape), _const_spec(wuv.shape),
        ],
        out_specs=out_specs,
        out_shape=out_shape,
        compiler_params=_cparams("arbitrary", "arbitrary"),
        name="mixer_inproj",
    )(x3, mod3, mod3, norm_g.reshape(1, d), w_in_a, cosr, sinr, rc, rsm, rsp, gqa, gkva, aq, ak, wuq, wuk, wuv)


def _group_norm(o, g):
    mu = jnp.mean(o, axis=-1, keepdims=True)
    oc = o - mu
    var = jnp.mean(oc * oc, axis=-1, keepdims=True)
    return oc * lax.rsqrt(var + EPS) * g


def _ret_prompt_kernel(q_ref, k_ref, v_ref, dec_ref, xi_ref, zeta_ref, gch_ref, gn_ref, o_ref, sout_ref, s_scr):
    c = pl.program_id(1)

    @pl.when(c == 0)
    def _():
        s_scr[...] = jnp.zeros_like(s_scr)

    for hh in range(RET_HEADS):
        q = q_ref[:, hh * RET_DK:(hh + 1) * RET_DK]
        k = k_ref[:, hh * RET_DK:(hh + 1) * RET_DK]
        v = v_ref[:, hh * RET_DV:(hh + 1) * RET_DV]
        s = s_scr[hh]
        scores = _dot_nt(q, k) * dec_ref[hh]
        inner = _dot(scores.astype(BF16), v)
        cross = _dot(q, s.astype(BF16)) * xi_ref[hh]
        kz = (k.astype(F32) * zeta_ref[hh]).astype(BF16)
        s_new = gch_ref[hh] * s + _dot_tn(kz, v)
        s_scr[hh] = s_new
        o_ref[:, hh * RET_DV:(hh + 1) * RET_DV] = _group_norm(inner + cross, gn_ref[hh]).astype(BF16)

    @pl.when(c == pl.num_programs(1) - 1)
    def _():
        sout_ref[...] = s_scr[...]


def _ret_prompt_call(rq, rk, rv, ret_tabs, gn_g):
    bx, lx, _ = rq.shape
    ch = RET_CHUNK
    dec, xi, zeta, gch = ret_tabs
    return pl.pallas_call(
        _ret_prompt_kernel,
        grid=(bx, lx // ch),
        in_specs=[
            pl.BlockSpec((None, ch, RET_HEADS * RET_DK), lambda b, c: (b, c, 0)),
            pl.BlockSpec((None, ch, RET_HEADS * RET_DK), lambda b, c: (b, c, 0)),
            pl.BlockSpec((None, ch, RET_HEADS * RET_DV), lambda b, c: (b, c, 0)),
            _const_spec(dec.shape), _const_spec(xi.shape), _const_spec(zeta.shape), _const_spec(gch.shape),
            _const_spec((RET_HEADS, 1, RET_DV)),
        ],
        out_specs=[
            pl.BlockSpec((None, ch, RET_HEADS * RET_DV), lambda b, c: (b, c, 0)),
            pl.BlockSpec((None, RET_HEADS, RET_DK, RET_DV), lambda b, c: (b, 0, 0, 0)),
        ],
        out_shape=[
            jax.ShapeDtypeStruct((bx, lx, RET_HEADS * RET_DV), BF16),
            jax.ShapeDtypeStruct((bx, RET_HEADS, RET_DK, RET_DV), F32),
        ],
        scratch_shapes=[pltpu.VMEM((RET_HEADS, RET_DK, RET_DV), F32)],
        compiler_params=_cparams("arbitrary", "arbitrary"),
        name="retention_prompt",
    )(rq, rk, rv, dec, xi, zeta, gch, gn_g.reshape(RET_HEADS, 1, RET_DV))


def _ret_sample_kernel(qt_ref, kt_ref, v_ref, s_ref, dec_ref, xi_ref, zeta_ref, gch_ref, gn_ref, o_ref, sout_ref, *, bb):
    for j in range(bb):
        for hh in range(RET_HEADS):
            qcol = qt_ref[hh * RET_DK:(hh + 1) * RET_DK, j:j + 1].astype(F32)
            kcol = kt_ref[hh * RET_DK:(hh + 1) * RET_DK, j:j + 1].astype(F32)
            vrow = v_ref[j:j + 1, hh * RET_DV:(hh + 1) * RET_DV].astype(F32)
            s0 = s_ref[j, hh]
            score = jnp.sum(qcol * kcol, axis=0, keepdims=True) * dec_ref[hh]
            inner = score * vrow
            cross = jnp.sum(qcol * s0, axis=0, keepdims=True) * xi_ref[hh]
            sout_ref[j, hh] = gch_ref[hh] * s0 + (kcol * zeta_ref[hh]) * vrow
            o_ref[j:j + 1, hh * RET_DV:(hh + 1) * RET_DV] = _group_norm(inner + cross, gn_ref[hh]).astype(BF16)


def _ret_sample_call(rq, rk, rv, state_all, layer, ret_tabs, gn_g, bb):
    n = rq.shape[0]
    dec, xi, zeta, gch = ret_tabs
    dqk = RET_HEADS * RET_DK
    qt = rq.reshape(n // bb, bb, dqk).transpose(0, 2, 1)
    kt = rk.reshape(n // bb, bb, dqk).transpose(0, 2, 1)
    return pl.pallas_call(
        functools.partial(_ret_sample_kernel, bb=bb),
        grid=(n // bb,),
        in_specs=[
            pl.BlockSpec((None, dqk, bb), lambda i: (i, 0, 0)),
            pl.BlockSpec((None, dqk, bb), lambda i: (i, 0, 0)),
            pl.BlockSpec((bb, RET_HEADS * RET_DV), lambda i: (i, 0)),
            pl.BlockSpec((None, bb, RET_HEADS, RET_DK, RET_DV), lambda i: (layer, i, 0, 0, 0)),
            _const_spec(dec.shape), _const_spec(xi.shape), _const_spec(zeta.shape), _const_spec(gch.shape),
            _const_spec((RET_HEADS, 1, RET_DV)),
        ],
        out_specs=[
            pl.BlockSpec((bb, RET_HEADS * RET_DV), lambda i: (i, 0)),
            pl.BlockSpec((bb, RET_HEADS, RET_DK, RET_DV), lambda i: (i, 0, 0, 0)),
        ],
        out_shape=[
            jax.ShapeDtypeStruct((n, RET_HEADS * RET_DV), BF16),
            jax.ShapeDtypeStruct(state_all.shape[1:], F32),
        ],
        compiler_params=_cparams("arbitrary"),
        name="retention_sample",
    )(qt, kt, rv, state_all, dec, xi, zeta, gch, gn_g.reshape(RET_HEADS, 1, RET_DV))


HEADS_PER_GROUP = 4


def _lane_tile(x, n):
    return x if n == 1 else jnp.concatenate([x] * n, axis=1)


def _attn_prompt_kernel(q_ref, k_ref, v_ref, o_ref, m_scr, l_scr, acc_scr, *, tq):
    i = pl.program_id(2)
    m_scr[...] = jnp.full_like(m_scr, NEG_INF)
    l_scr[...] = jnp.zeros_like(l_scr)
    acc_scr[...] = jnp.zeros_like(acc_scr)
    srep = tq // LANES
    vrep = acc_scr.shape[-1] // LANES

    def tile(kt, bias):
        ks = pl.multiple_of(kt * tq, tq)
        vblk = v_ref[pl.ds(ks, tq), :]
        for hh in range(HEADS_PER_GROUP):
            q = q_ref[:, hh * HEAD_SLOT:(hh + 1) * HEAD_SLOT]
            k = k_ref[pl.ds(ks, tq), hh * HEAD_SLOT:(hh + 1) * HEAD_SLOT]
            s = _dot_nt(q, k)
            if bias is not None:
                s = s + bias
            m_old = m_scr[hh]
            m_new = jnp.maximum(m_old, jnp.max(s, axis=-1, keepdims=True))
            alpha = jnp.exp2(m_old - m_new)
            p = jnp.exp2(s - _lane_tile(m_new, srep))
            l_scr[hh] = alpha * l_scr[hh] + jnp.sum(p, axis=-1, keepdims=True)
            acc_scr[hh] = _lane_tile(alpha, vrep) * acc_scr[hh] + _dot(p.astype(BF16), vblk)
            m_scr[hh] = m_new

    def body(kt, carry):
        tile(kt, None)
        return carry

    lax.fori_loop(0, i, body, 0)
    row = lax.broadcasted_iota(jnp.int32, (tq, tq), 0)
    col = lax.broadcasted_iota(jnp.int32, (tq, tq), 1)
    tile(i, jnp.where(row >= col, 0.0, NEG_INF).astype(F32))

    lane = lax.broadcasted_iota(jnp.int32, o_ref.shape, 1)
    o = jnp.zeros(o_ref.shape, F32)
    for hh in range(HEADS_PER_GROUP):
        oh = acc_scr[hh] * _lane_tile(1.0 / l_scr[hh], vrep)
        o = jnp.where((lane >= hh * V_DIM) & (lane < (hh + 1) * V_DIM), oh, o)
    o_ref[...] = o.astype(BF16)


def _attn_prompt_call(q, k, v, tq):
    bx, lx, _ = q.shape
    gw = HEADS_PER_GROUP * HEAD_SLOT
    vw = HEADS_PER_GROUP * V_DIM
    ng = MLA_HEADS // HEADS_PER_GROUP
    return pl.pallas_call(
        functools.partial(_attn_prompt_kernel, tq=tq),
        grid=(bx, ng, lx // tq),
        in_specs=[
            pl.BlockSpec((None, tq, gw), lambda b, g, i: (b, i, g)),
            pl.BlockSpec((None, lx, gw), lambda b, g, i: (b, 0, g)),
            pl.BlockSpec((None, lx, vw), lambda b, g, i: (b, 0, g)),
        ],
        out_specs=pl.BlockSpec((None, tq, vw), lambda b, g, i: (b, i, g)),
        out_shape=jax.ShapeDtypeStruct((bx, lx, MLA_HEADS * V_DIM), BF16),
        scratch_shapes=[
            pltpu.VMEM((HEADS_PER_GROUP, tq, LANES), F32),
            pltpu.VMEM((HEADS_PER_GROUP, tq, LANES), F32),
            pltpu.VMEM((HEADS_PER_GROUP, tq, vw), F32),
        ],
        compiler_params=_cparams("arbitrary", "arbitrary", "arbitrary"),
        name="attention_prompt",
    )(q, k, v)


def _qlat_kernel(q_ref, w_ref, o_ref):
    o_ref[...] = _dot(q_ref[...], w_ref[...]).astype(BF16)


def _qlat_call(q2, wukt):
    n = q2.shape[0]
    r = wukt.shape[2]
    return pl.pallas_call(
        _qlat_kernel,
        grid=(MLA_HEADS,),
        in_specs=[
            pl.BlockSpec((n, HEAD_SLOT), lambda h: (0, h)),
            pl.BlockSpec((None, HEAD_SLOT, r), lambda h: (h, 0, 0)),
        ],
        out_specs=pl.BlockSpec((n, r), lambda h: (0, h)),
        out_shape=jax.ShapeDtypeStruct((n, MLA_HEADS * r), BF16),
        compiler_params=_cparams("arbitrary"),
        name="absorb_q",
    )(q2, wukt)


def _attn_paged_kernel(pt_ref, qlat_ref, q_ref, cnew_ref, pnew_ref, snew_ref, ckv_hbm, kpe_hbm, ks_hbm, o_ref,
                       ckv_buf, kpe_buf, ks_buf, sem, *, layer, cp, n_chunk):
    b = pl.program_id(0)
    nb = pl.num_programs(0)

    def page_copies(bi, ci, slot, g):
        page = pt_ref[bi, ci * cp + g]
        lanes = pl.ds(pl.multiple_of(g * PAGE_SIZE, PAGE_SIZE), PAGE_SIZE)
        return (pltpu.make_async_copy(ckv_hbm.at[layer, page], ckv_buf.at[slot, g], sem.at[slot, 0]),
                pltpu.make_async_copy(kpe_hbm.at[layer, page], kpe_buf.at[slot, :, lanes], sem.at[slot, 1]),
                pltpu.make_async_copy(ks_hbm.at[layer, page], ks_buf.at[slot, :, lanes], sem.at[slot, 2]))

    def start(bi, ci, slot):
        def one(g, carry):
            for c in page_copies(bi, ci, slot, g):
                c.start()
            return carry
        lax.fori_loop(0, cp, one, 0)

    def wait(bi, ci, slot):
        def one(g, carry):
            for c in page_copies(bi, ci, slot, g):
                c.wait()
            return carry
        lax.fori_loop(0, cp, one, 0)

    @pl.when(b == 0)
    def _():
        start(0, 0, 0)

    qlat = qlat_ref[...]
    qpe = q_ref[:, NOPE_DIM:QK_DIM]
    h, r = qlat.shape
    m = jnp.full((h, 1), NEG_INF, F32)
    l = jnp.zeros((h, 1), F32)
    acc = jnp.zeros((h, r), F32)

    for ci in range(n_chunk):
        slot = ci % 2
        if ci + 1 < n_chunk:
            start(b, ci + 1, 1 - slot)
        else:
            @pl.when(b + 1 < nb)
            def _():
                start(b + 1, 0, 0)
        wait(b, ci, slot)
        ckv_bf = ckv_buf[slot].reshape(cp * PAGE_SIZE, r).astype(BF16)
        s = (_dot_nt(qlat, ckv_bf) + _dot(qpe, kpe_buf[slot].astype(BF16))) * ks_buf[slot]
        m_new = jnp.maximum(m, jnp.max(s, axis=-1, keepdims=True))
        alpha = jnp.exp2(m - m_new)
        p = jnp.exp2(s - m_new)
        l = alpha * l + jnp.sum(p, axis=-1, keepdims=True)
        acc = alpha * acc + _dot(p.astype(BF16), ckv_bf)
        m = m_new

    cn = cnew_ref[...]
    s_new = (jnp.sum(qlat.astype(F32) * cn, axis=-1, keepdims=True)
             + jnp.sum(qpe.astype(F32) * pnew_ref[...], axis=-1, keepdims=True)) * snew_ref[...]
    m_new = jnp.maximum(m, s_new)
    alpha = jnp.exp2(m - m_new)
    p_new = jnp.exp2(s_new - m_new)
    l = alpha * l + p_new
    acc = alpha * acc + p_new * cn
    o_ref[...] = (acc / l).astype(BF16)


def _attn_paged_call(page_table, qlat3, q3, cnew, pnew, snew_t, ckv_pool, kpe_pool_t, ks_pool_t, layer, cp):
    nb, n_pages = page_table.shape
    assert n_pages % (2 * cp) == 0, "chunk count must be even so buffer slots alternate across sequences"
    n_chunk = n_pages // cp
    r = ckv_pool.shape[-1]
    grid_spec = pltpu.PrefetchScalarGridSpec(
        num_scalar_prefetch=1,
        grid=(nb,),
        in_specs=[
            pl.BlockSpec((None, MLA_HEADS, r), lambda b, pt: (b, 0, 0)),
            pl.BlockSpec((None, MLA_HEADS, HEAD_SLOT), lambda b, pt: (b, 0, 0)),
            pl.BlockSpec((None, 1, r), lambda b, pt: (b, 0, 0)),
            pl.BlockSpec((None, 1, ROPE_DIM), lambda b, pt: (b, 0, 0)),
            pl.BlockSpec((None, MLA_HEADS, 1), lambda b, pt: (b, 0, 0)),
            pl.BlockSpec(memory_space=pl.ANY),
            pl.BlockSpec(memory_space=pl.ANY),
            pl.BlockSpec(memory_space=pl.ANY),
        ],
        out_specs=pl.BlockSpec((None, MLA_HEADS, r), lambda b, pt: (b, 0, 0)),
        scratch_shapes=[
            pltpu.VMEM((2, cp, PAGE_SIZE, r), F32),
            pltpu.VMEM((2, ROPE_DIM, cp * PAGE_SIZE), F32),
            pltpu.VMEM((2, MLA_HEADS, cp * PAGE_SIZE), F32),
            pltpu.SemaphoreType.DMA((2, 3)),
        ],
    )
    return pl.pallas_call(
        functools.partial(_attn_paged_kernel, layer=layer, cp=cp, n_chunk=n_chunk),
        grid_spec=grid_spec,
        out_shape=jax.ShapeDtypeStruct((nb, MLA_HEADS, r), BF16),
        compiler_params=_cparams("arbitrary"),
        name="attention_paged",
    )(page_table, qlat3, q3, cnew, pnew, snew_t, ckv_pool, kpe_pool_t, ks_pool_t)


def _uv_pair_kernel(o_ref, w_ref, y_ref):
    r = w_ref.shape[1]
    parts = [_dot(o_ref[:, j * r:(j + 1) * r], w_ref[j]) for j in range(w_ref.shape[0])]
    y_ref[...] = jnp.concatenate(parts, axis=-1).astype(BF16)


def _uv_call(olat2, wuv_h):
    n = olat2.shape[0]
    r = wuv_h.shape[1]
    hpb = LANES // V_DIM
    return pl.pallas_call(
        _uv_pair_kernel,
        grid=(MLA_HEADS // hpb,),
        in_specs=[
            pl.BlockSpec((n, hpb * r), lambda h: (0, h)),
            pl.BlockSpec((hpb, r, V_DIM), lambda h: (h, 0, 0)),
        ],
        out_specs=pl.BlockSpec((n, hpb * V_DIM), lambda h: (0, h)),
        out_shape=jax.ShapeDtypeStruct((n, MLA_HEADS * V_DIM), BF16),
        compiler_params=_cparams("arbitrary"),
        name="value_up",
    )(olat2, wuv_h)


def _outproj_kernel(x_ref, sh_ref, sc_ref, gt_ref, g_ref, wg_ref, ret_ref, att_ref, wpa_ref, wpb_ref, wo_ref, o_ref):
    x = x_ref[...]
    d = x.shape[-1]
    h = _modulated_norm(x, g_ref[...], sh_ref[...], sc_ref[...]).astype(BF16)
    gates = _dot(h, wg_ref[...])
    nv = ret_ref.shape[-1]
    rg, ga, gb = gates[:, :nv], gates[:, nv:nv + d], gates[:, nv + d:]
    ya = _dot((ret_ref[...].astype(F32) * (rg * jax.nn.sigmoid(rg))).astype(BF16), wpa_ref[...])
    yb = _dot(att_ref[...], wpb_ref[...])
    merged = jax.nn.sigmoid(ga) * ya + jax.nn.sigmoid(gb) * yb
    o_ref[...] = x + gt_ref[...] * _dot(merged.astype(BF16), wo_ref[...])


def _outproj_call(x3, mod3, norm_g, w_gates, ret_o, att_o, wpa, wpb, wo, tl):
    bx, lx, d = x3.shape
    lm = mod3.shape[1]
    lm_blk = 1 if lm == 1 else tl

    def tok(n):
        return pl.BlockSpec((None, tl, n), lambda b, i: (b, i, 0))

    return pl.pallas_call(
        _outproj_kernel,
        grid=(bx, lx // tl),
        in_specs=[
            tok(d), _mod_spec(lm_blk, d, 3), _mod_spec(lm_blk, d, 4), _mod_spec(lm_blk, d, 5), _const_spec((1, d)),
            _const_spec(w_gates.shape), tok(ret_o.shape[-1]), tok(att_o.shape[-1]),
            _const_spec(wpa.shape), _const_spec(wpb.shape), _const_spec(wo.shape),
        ],
        out_specs=tok(d),
        out_shape=jax.ShapeDtypeStruct(x3.shape, F32),
        compiler_params=_cparams("arbitrary", "arbitrary"),
        name="mixer_outproj",
    )(x3, mod3, mod3, mod3, norm_g.reshape(1, d), w_gates, ret_o, att_o, wpa, wpb, wo)


def _rope_tables(pos):
    half_r = RET_DK // 2
    fr = ROPE_THETA ** (-jnp.arange(half_r, dtype=F32) / half_r)
    ang = pos[:, None] * fr[None, :]
    cos, sin = jnp.cos(ang), jnp.sin(ang)
    cosr = jnp.concatenate([cos, cos], axis=-1)
    sinr = jnp.concatenate([-sin, sin], axis=-1)
    half_m = ROPE_DIM // 2
    fm = ROPE_THETA ** (-jnp.arange(half_m, dtype=F32) / half_m)
    angm = pos[:, None] * fm[None, :]
    cm, sm = jnp.cos(angm), jnp.sin(angm)
    n = pos.shape[0]
    z16, z32 = jnp.zeros((n, half_m), F32), jnp.zeros((n, LANES - QK_DIM), F32)
    rc = jnp.concatenate([jnp.ones((n, NOPE_DIM), F32), cm, cm, z32], axis=-1)
    rsm = jnp.concatenate([jnp.zeros((n, NOPE_DIM), F32), -sm, z16, z32], axis=-1)
    rsp = jnp.concatenate([jnp.zeros((n, NOPE_DIM), F32), z16, sm, z32], axis=-1)
    return cosr, sinr, rc, rsm, rsp


def _retention_tables(chunk):
    log_gamma = jnp.log1p(-jnp.exp2(-5.0 - jnp.arange(RET_HEADS, dtype=F32)))
    idx = jnp.arange(chunk, dtype=F32)
    rel = idx[:, None] - idx[None, :]
    decay = jnp.where(rel[None] >= 0, jnp.exp(rel[None] * log_gamma[:, None, None]), 0.0)
    xi = jnp.exp((idx[None, :] + 1.0) * log_gamma[:, None])[:, :, None]
    zeta = jnp.exp((chunk - 1.0 - idx)[None, :] * log_gamma[:, None])[:, :, None]
    gch = jnp.exp(chunk * log_gamma)[:, None, None]
    return decay, xi, zeta, gch


def _pad_heads(w, width):
    r, hn, n = w.shape
    return jnp.pad(w, ((0, 0), (0, 0), (0, width - n))).reshape(r, hn * width)


def _prep_layer(l, w_in, ffn_w13, ffn_w2, w_uq, w_uk, w_uv, w_pa, w_pb, w_out, mla_qa_g, mla_kva_g, mla_qn_g, mla_kn_g):
    d = w_in.shape[1]
    rqk, rvw = RET_HEADS * RET_DK, RET_HEADS * RET_DV
    q_rank, kv_rank = w_uq.shape[1], w_uk.shape[1]
    o = np.cumsum([0, rqk, rqk, rvw, rvw, q_rank, kv_rank, ROPE_DIM, d, d])
    wi = w_in[l]
    rq, rk, rv, rg, cq, ckv, kpe, ga, gb = [wi[:, o[i]:o[i + 1]] for i in range(9)]
    kpe_slot = jnp.pad(kpe, ((0, 0), (NOPE_DIM, LANES - QK_DIM)))
    w_a = jnp.concatenate([rv, rq, rk, cq, ckv, kpe_slot], axis=1).astype(BF16)
    offs = tuple(int(v) for v in np.cumsum([0, rvw, rqk, rqk, q_rank, kv_rank, LANES]))
    w_g = jnp.concatenate([rg, ga, gb], axis=1).astype(BF16)
    wuq = _pad_heads(w_uq[l], HEAD_SLOT).astype(BF16)
    wuk = _pad_heads(w_uk[l], HEAD_SLOT).astype(BF16)
    wuv = w_uv[l].reshape(kv_rank, MLA_HEADS * V_DIM).astype(BF16)
    wukt = jnp.pad(jnp.transpose(w_uk[l], (1, 2, 0)), ((0, 0), (0, HEAD_SLOT - NOPE_DIM), (0, 0))).astype(BF16)
    wuv_h = jnp.transpose(w_uv[l], (1, 0, 2)).astype(BF16)
    gqn, gkn = mla_qn_g[l], mla_kn_g[l]
    zpad = jnp.zeros((LANES - QK_DIM,), F32)
    aq = (jnp.concatenate([gqn[:NOPE_DIM] * gkn[:NOPE_DIM], gqn[NOPE_DIM:], zpad]) * (MLA_SCALE * LOG2E)).reshape(1, LANES)
    ak = jnp.concatenate([jnp.zeros((NOPE_DIM,), F32), gkn[NOPE_DIM:], zpad]).reshape(1, LANES)
    gains = (mla_qa_g[l].reshape(1, q_rank), mla_kva_g[l].reshape(1, kv_rank), aq, ak)
    return dict(w_a=w_a, offs=offs, w_g=w_g, wuq=wuq, wuk=wuk, wuv=wuv, wukt=wukt, wuv_h=wuv_h, gains=gains,
                w13=[ffn_w13[l, j].astype(BF16) for j in range(2)], w2=[ffn_w2[l, j].astype(BF16) for j in range(2)],
                wpa=w_pa[l].astype(BF16), wpb=w_pb[l].astype(BF16), wo=w_out[l].astype(BF16))


def _largest_tile(n, cap):
    t = min(n, cap)
    while n % t:
        t //= 2
    return t


def kernel(x_prompt, x_sample, cache_ckv, cache_kpe, cache_kscale, state_ret, page_table, c_prompt, c_sample, ada_w, ada_b, norm_g, ffn_w13, ffn_w2, w_in, ret_gn_g, mla_qa_g, mla_kva_g, w_uq, mla_qn_g, mla_kn_g, w_uk, w_uv, w_pa, w_pb, w_out):
    depth = ada_w.shape[0]
    bp, lp, d = x_prompt.shape
    ns, ls, _ = x_sample.shape
    assert ls == 1, "sample group is one new token per sequence"
    assert lp % RET_CHUNK == 0
    n_pages = page_table.shape[1]
    past_len = n_pages * PAGE_SIZE

    mod = _mod_call(jnp.concatenate([c_prompt, c_sample], axis=0), ada_w.astype(BF16), ada_b)
    tabs_p = _rope_tables(jnp.arange(lp, dtype=F32))
    tabs_s = _rope_tables(jnp.full((ns,), float(past_len), F32))
    rt_p = _retention_tables(RET_CHUNK)
    rt_s = _retention_tables(1)

    tl_p = _largest_tile(lp, 512)
    tl_in = _largest_tile(lp, 256)
    tq = _largest_tile(lp, 512)
    bb = _largest_tile(ns, 8)
    cp = _largest_tile(n_pages // 2, 64)
    kpe_pool_t = jnp.swapaxes(cache_kpe, 2, 3)
    ks_pool_t = jnp.swapaxes(cache_kscale, 2, 3)

    yp = x_prompt
    ys = x_sample.reshape(1, ns, d)
    st_p, st_s = [], []
    for l in range(depth):
        w = _prep_layer(l, w_in, ffn_w13, ffn_w2, w_uq, w_uk, w_uv, w_pa, w_pb, w_out,
                        mla_qa_g, mla_kva_g, mla_qn_g, mla_kn_g)
        mod_p = mod[l, :bp].reshape(bp, 1, N_MOD * d)
        mod_s = mod[l, bp:].reshape(1, ns, N_MOD * d)
        ng = norm_g[l]

        yp = _ffn_call(yp, mod_p, 0, ng[0], w["w13"][0], w["w2"][0], tl_p)
        rq, rk, rv, q, ckv, kpe, ks, kf, vf = _inproj_call(
            yp, mod_p, ng[1], w["w_a"], w["offs"], tabs_p, w["gains"], w["wuq"], w["wuk"], w["wuv"], tl_in, True)
        ret_o, ret_s = _ret_prompt_call(rq, rk, rv, rt_p, ret_gn_g[l])
        att_o = _attn_prompt_call(q, kf, vf, tq)
        yp = _outproj_call(yp, mod_p, ng[1], w["w_g"], ret_o, att_o, w["wpa"], w["wpb"], w["wo"], tl_p)
        yp = _ffn_call(yp, mod_p, 6, ng[2], w["w13"][1], w["w2"][1], tl_p)
        st_p.append((ckv, kpe, ks, ret_s))

        ys = _ffn_call(ys, mod_s, 0, ng[0], w["w13"][0], w["w2"][0], ns)
        rq, rk, rv, q, ckv, kpe, ks = _inproj_call(
            ys, mod_s, ng[1], w["w_a"], w["offs"], tabs_s, w["gains"], w["wuq"], w["wuk"], w["wuv"], ns, False)
        ret_o, ret_s = _ret_sample_call(rq[0], rk[0], rv[0], state_ret, l, rt_s, ret_gn_g[l], bb)
        qlat = _qlat_call(q[0], w["wukt"])

        olat = _attn_paged_call(page_table, qlat.reshape(ns, MLA_HEADS, -1), q[0].reshape(ns, MLA_HEADS, HEAD_SLOT),
                                ckv.reshape(ns, 1, -1), kpe.reshape(ns, 1, -1), ks.reshape(ns, MLA_HEADS, 1),
                                cache_ckv, kpe_pool_t, ks_pool_t, l, cp)
        att_o = _uv_call(olat.reshape(ns, -1), w["wuv_h"])
        ys = _outproj_call(ys, mod_s, ng[1], w["w_g"], ret_o[None], att_o[None], w["wpa"], w["wpb"], w["wo"], ns)
        ys = _ffn_call(ys, mod_s, 6, ng[2], w["w13"][1], w["w2"][1], ns)
        st_s.append((ckv.reshape(ns, 1, -1), kpe.reshape(ns, 1, -1), ks.reshape(ns, 1, -1), ret_s))

    def stack(states, i):
        return jnp.stack([st[i] for st in states], axis=0)

    return (yp, ys.reshape(ns, 1, d),
            stack(st_p, 0), stack(st_p, 1), stack(st_p, 2), stack(st_p, 3),
            stack(st_s, 0), stack(st_s, 1), stack(st_s, 2), stack(st_s, 3))
```

```python
import functools

import jax
import jax.numpy as jnp
import numpy as np
from jax import lax
from jax.experimental import pallas as pl
from jax.experimental.pallas import tpu as pltpu

F32 = jnp.float32
BF16 = jnp.bfloat16

RET_HEADS = 4
RET_DK = 128
RET_DV = 256
RET_CHUNK = 128
MLA_HEADS = 16
NOPE_DIM = 64
ROPE_DIM = 32
V_DIM = 64
QK_DIM = NOPE_DIM + ROPE_DIM
MLA_SCALE = QK_DIM ** -0.5
LOG2E = float(np.log2(np.e))
ROPE_THETA = 10000.0
N_MOD = 9
EPS = 1e-6
PAGE_SIZE = 128

LANES = 128
MXU_DEPTH = 256
HEAD_SLOT = LANES
VMEM_LIMIT = 56 * 1024 * 1024
NEG_INF = float("-inf")


def _cparams(*sem):
    return pltpu.CompilerParams(dimension_semantics=sem, vmem_limit_bytes=VMEM_LIMIT)


def _dot(a, b):
    return jnp.dot(a, b, preferred_element_type=F32)


def _dot_nt(a, b):
    return lax.dot_general(a, b, (((1,), (1,)), ((), ())), preferred_element_type=F32)


def _dot_tn(a, b):
    return lax.dot_general(a, b, (((0,), (0,)), ((), ())), preferred_element_type=F32)


def _const_spec(shape):
    n = len(shape)
    return pl.BlockSpec(shape, lambda *_: (0,) * n)


def _modulated_norm(x, g, shift, scale):
    ms = jnp.mean(x * x, axis=-1, keepdims=True)
    y = x * lax.rsqrt(ms + EPS) * g
    return y * (1.0 + scale) + shift


def _mod_spec(lm_blk, d, k):
    if lm_blk == 1:
        return pl.BlockSpec((None, 1, d), lambda b, i: (b, 0, k))
    return pl.BlockSpec((None, lm_blk, d), lambda b, i: (b, i, k))


def _mod_kernel(c_ref, w_ref, b_ref, o_ref):
    c = c_ref[...]
    s = (c * jax.nn.sigmoid(c)).astype(BF16)
    o_ref[...] = _dot(s, w_ref[...].astype(BF16)) + b_ref[...]


def _mod_call(c_all, ada_w, ada_b):
    depth, d, nd = ada_w.shape
    n = c_all.shape[0]
    return pl.pallas_call(
        _mod_kernel,
        grid=(depth, nd // d),
        in_specs=[
            pl.BlockSpec((n, d), lambda l, j: (0, 0)),
            pl.BlockSpec((None, d, d), lambda l, j: (l, 0, j)),
            pl.BlockSpec((None, 1, d), lambda l, j: (l, 0, j)),
        ],
        out_specs=pl.BlockSpec((None, n, d), lambda l, j: (l, 0, j)),
        out_shape=jax.ShapeDtypeStruct((depth, n, nd), F32),
        compiler_params=_cparams("arbitrary", "arbitrary"),
        name="adaln_mod",
    )(c_all, ada_w, ada_b.reshape(depth, 1, nd))


def _ffn_kernel(x_ref, sh_ref, sc_ref, gt_ref, g_ref, w13_ref, w2_ref, o_ref, *, d_ff):
    x = x_ref[...]
    h = _modulated_norm(x, g_ref[...], sh_ref[...], sc_ref[...]).astype(BF16)
    ab = _dot(h, w13_ref[...])
    a, b = ab[:, :d_ff], ab[:, d_ff:]
    u = (a * jax.nn.sigmoid(a) * b).astype(BF16)
    y = _dot(u, w2_ref[...])
    o_ref[...] = x + 0.5 * gt_ref[...] * y


def _ffn_call(x3, mod3, k0, norm_g, w13_all, w2_all, layer, j, tl):
    bx, lx, d = x3.shape
    lm = mod3.shape[1]
    lm_blk = 1 if lm == 1 else tl
    d_ff = w2_all.shape[2]
    return pl.pallas_call(
        functools.partial(_ffn_kernel, d_ff=d_ff),
        grid=(bx, lx // tl),
        in_specs=[
            pl.BlockSpec((None, tl, d), lambda b, i: (b, i, 0)),
            _mod_spec(lm_blk, d, k0), _mod_spec(lm_blk, d, k0 + 1), _mod_spec(lm_blk, d, k0 + 2),
            _const_spec((1, d)),
            pl.BlockSpec((None, None) + w13_all.shape[2:], lambda b, i: (layer, j, 0, 0), pipeline_mode=pl.Buffered(1)),
            pl.BlockSpec((None, None) + w2_all.shape[2:], lambda b, i: (layer, j, 0, 0), pipeline_mode=pl.Buffered(1)),
        ],
        out_specs=pl.BlockSpec((None, tl, d), lambda b, i: (b, i, 0)),
        out_shape=jax.ShapeDtypeStruct(x3.shape, F32),
        compiler_params=_cparams("arbitrary", "arbitrary"),
        name="ffn",
    )(x3, mod3, mod3, mod3, norm_g.reshape(1, d), w13_all, w2_all)


def _rope_slot(t, c, sm, sp):
    return t * c + pltpu.roll(t, LANES - ROPE_DIM // 2, axis=1) * sm + pltpu.roll(t, ROPE_DIM // 2, axis=1) * sp


def _slot_sumsq(v, ones_ref):
    w = ones_ref.shape[0]
    ones_bd = ones_ref[...]
    return jnp.concatenate([_dot((v[:, j:j + w] * v[:, j:j + w]).astype(BF16), ones_bd)
                            for j in range(0, v.shape[1], w)], axis=1)


def _inproj_kernel(x_ref, sh_ref, sc_ref, g_ref, w_ref, cosr_ref, sinr_ref, rc_ref, rsm_ref, rsp_ref,
                   gqa_ref, gkva_ref, aq_ref, ak_ref, wuq_ref, wuk_ref, wuv_ref, ones_ref,
                   rq_ref, rk_ref, rv_ref, q_ref, ckv_ref, kpe_ref, ks_ref, *kv_refs, offs):
    _inproj_rows(slice(0, x_ref.shape[0]), x_ref, sh_ref, sc_ref, g_ref, w_ref, cosr_ref, sinr_ref,
                 rc_ref, rsm_ref, rsp_ref, gqa_ref, gkva_ref, aq_ref, ak_ref, wuq_ref, wuk_ref, wuv_ref, ones_ref,
                 rq_ref, rk_ref, rv_ref, q_ref, ckv_ref, kpe_ref, ks_ref, kv_refs, offs)


def _inproj_rows(rows, x_ref, sh_ref, sc_ref, g_ref, w_ref, cosr_ref, sinr_ref, rc_ref, rsm_ref, rsp_ref,
                 gqa_ref, gkva_ref, aq_ref, ak_ref, wuq_ref, wuk_ref, wuv_ref, ones_ref,
                 rq_ref, rk_ref, rv_ref, q_ref, ckv_ref, kpe_ref, ks_ref, kv_refs, offs):
    o_rv, o_rq, o_rk, o_cq, o_ckv, o_kpe, n_in = offs

    def mod_rows(ref):
        return ref[...] if ref.shape[0] == 1 else ref[rows, :]

    x = x_ref[rows, :]
    h = _modulated_norm(x, g_ref[...], mod_rows(sh_ref), mod_rows(sc_ref)).astype(BF16)
    p = _dot(h, w_ref[...])

    rv_ref[rows, :] = p[:, o_rv:o_rq].astype(BF16)
    cosr, sinr = cosr_ref[rows, :], sinr_ref[rows, :]
    for hh in range(RET_HEADS):
        lo, hi = hh * RET_DK, (hh + 1) * RET_DK
        tq = p[:, o_rq + lo:o_rq + hi]
        rq_ref[rows, lo:hi] = (tq * cosr + pltpu.roll(tq, RET_DK // 2, axis=1) * sinr).astype(BF16)
        tk = p[:, o_rk + lo:o_rk + hi]
        rk_ref[rows, lo:hi] = ((tk * cosr + pltpu.roll(tk, RET_DK // 2, axis=1) * sinr) * (RET_DK ** -0.5)).astype(BF16)

    rc, rsm, rsp = rc_ref[rows, :], rsm_ref[rows, :], rsp_ref[rows, :]

    cq = p[:, o_cq:o_ckv]
    cq = (cq * lax.rsqrt(jnp.mean(cq * cq, axis=-1, keepdims=True) + EPS) * gqa_ref[...]).astype(BF16)
    qf = _dot(cq, wuq_ref[...])
    qn = qf * lax.rsqrt(_slot_sumsq(qf, ones_ref) * (1.0 / QK_DIM) + EPS) * aq_ref[...]
    for hh in range(MLA_HEADS):
        lo, hi = hh * HEAD_SLOT, (hh + 1) * HEAD_SLOT
        q_ref[rows, lo:hi] = _rope_slot(qn[:, lo:hi], rc, rsm, rsp).astype(BF16)

    craw = p[:, o_ckv:o_kpe]
    ckv = craw * lax.rsqrt(jnp.mean(craw * craw, axis=-1, keepdims=True) + EPS) * gkva_ref[...]
    ckv_ref[rows, :] = ckv
    ckv_bf = ckv.astype(BF16)
    kn = _dot(ckv_bf, wuk_ref[...])
    kblk = p[:, o_kpe:n_in]
    kpe_ss = _dot((kblk * kblk).astype(BF16), ones_ref[:HEAD_SLOT, :HEAD_SLOT])
    kpe_rot = _rope_slot(kblk * ak_ref[...], rc, rsm, rsp)
    kpe_ref[rows, :] = kpe_rot[:, NOPE_DIM:QK_DIM]
    ms = (_slot_sumsq(kn, ones_ref) + _lane_tile(kpe_ss, MLA_HEADS)) * (1.0 / QK_DIM)
    ks_rep = lax.rsqrt(ms + EPS)
    lane16 = lax.broadcasted_iota(jnp.int32, (x.shape[0], MLA_HEADS), 1)
    ks_all = jnp.zeros((x.shape[0], MLA_HEADS), F32)
    for hh in range(MLA_HEADS):
        ks_all = jnp.where(lane16 == hh, ks_rep[:, hh * HEAD_SLOT:hh * HEAD_SLOT + MLA_HEADS], ks_all)
    ks_ref[rows, :] = ks_all
    if kv_refs:
        kv_refs[0][rows, :] = ((kn + _lane_tile(kpe_rot, MLA_HEADS)) * ks_rep).astype(BF16)
        kv_refs[1][rows, :] = _dot(ckv_bf, wuv_ref[...]).astype(BF16)


def _inproj_call(x3, mod3, norm_g, w_in_a, offs, tabs, gains, wuq, wuk, wuv, tl, emit_kv):
    bx, lx, d = x3.shape
    lm = mod3.shape[1]
    lm_blk = 1 if lm == 1 else tl
    cosr, sinr, rc, rsm, rsp = tabs
    gqa, gkva, aq, ak = gains
    hs = MLA_HEADS * HEAD_SLOT
    aq = jnp.tile(aq, (1, MLA_HEADS))
    slot_id = jnp.arange(MXU_DEPTH) // HEAD_SLOT
    ones_bd = (slot_id[:, None] == slot_id[None, :]).astype(BF16)

    def tok(n):
        return pl.BlockSpec((None, tl, n), lambda b, i: (b, i, 0))

    def tab():
        return pl.BlockSpec((tl, LANES), lambda b, i: (i, 0))

    def out(n, dt):
        return jax.ShapeDtypeStruct((bx, lx, n), dt)

    out_specs = [tok(RET_HEADS * RET_DK), tok(RET_HEADS * RET_DK), tok(RET_HEADS * RET_DV), tok(hs),
                 tok(wuk.shape[0]), tok(ROPE_DIM), tok(MLA_HEADS)]
    out_shape = [out(RET_HEADS * RET_DK, BF16), out(RET_HEADS * RET_DK, BF16), out(RET_HEADS * RET_DV, BF16),
                 out(hs, BF16), out(wuk.shape[0], F32), out(ROPE_DIM, F32), out(MLA_HEADS, F32)]
    if emit_kv:
        out_specs += [tok(hs), tok(MLA_HEADS * V_DIM)]
        out_shape += [out(hs, BF16), out(MLA_HEADS * V_DIM, BF16)]
    return pl.pallas_call(
        functools.partial(_inproj_kernel, offs=offs),
        grid=(bx, lx // tl),
        in_specs=[
            tok(d), _mod_spec(lm_blk, d, 3), _mod_spec(lm_blk, d, 4), _const_spec((1, d)),
            _const_spec(w_in_a.shape), tab(), tab(), tab(), tab(), tab(),
            _const_spec(gqa.shape), _const_spec(gkva.shape), _const_spec(aq.shape), _const_spec(ak.shape),
            _const_spec(wuq.shape), _const_spec(wuk.shape), _const_spec(wuv.shape), _const_spec(ones_bd.shape),
        ],
        out_specs=out_specs,
        out_shape=out_shape,
        compiler_params=_cparams("arbitrary", "arbitrary"),
        name="mixer_inproj",
    )(x3, mod3, mod3, norm_g.reshape(1, d), w_in_a, cosr, sinr, rc, rsm, rsp, gqa, gkva, aq, ak, wuq, wuk, wuv,
      ones_bd)


def _group_norm(o, g):
    mu = jnp.mean(o, axis=-1, keepdims=True)
    oc = o - mu
    var = jnp.mean(oc * oc, axis=-1, keepdims=True)
    return oc * lax.rsqrt(var + EPS) * g


def _ret_prompt_kernel(q_ref, k_ref, v_ref, dec_ref, xi_ref, zeta_ref, gch_ref, gn_ref, o_ref, sout_ref, s_scr):
    c = pl.program_id(1)

    @pl.when(c == 0)
    def _():
        s_scr[...] = jnp.zeros_like(s_scr)

    for bi in range(q_ref.shape[0]):
        for hh in range(RET_HEADS):
            q = q_ref[bi, :, hh * RET_DK:(hh + 1) * RET_DK]
            k = k_ref[bi, :, hh * RET_DK:(hh + 1) * RET_DK]
            v = v_ref[bi, :, hh * RET_DV:(hh + 1) * RET_DV]
            s = s_scr[bi, hh]
            scores = _dot_nt(q, k) * dec_ref[hh]
            inner = _dot(scores.astype(BF16), v)
            cross = _dot(q, s.astype(BF16)) * xi_ref[hh]
            kz = (k.astype(F32) * zeta_ref[hh]).astype(BF16)
            s_new = gch_ref[hh] * s + _dot_tn(kz, v)
            s_scr[bi, hh] = s_new
            o_ref[bi, :, hh * RET_DV:(hh + 1) * RET_DV] = _group_norm(inner + cross, gn_ref[hh]).astype(BF16)

    @pl.when(c == pl.num_programs(1) - 1)
    def _():
        sout_ref[...] = s_scr[...]


def _ret_prompt_call(rq, rk, rv, ret_tabs, gn_g, nbt):
    bx, lx, _ = rq.shape
    ch = RET_CHUNK
    dec, xi, zeta, gch = ret_tabs
    return pl.pallas_call(
        _ret_prompt_kernel,
        grid=(bx // nbt, lx // ch),
        in_specs=[
            pl.BlockSpec((nbt, ch, RET_HEADS * RET_DK), lambda b, c: (b, c, 0)),
            pl.BlockSpec((nbt, ch, RET_HEADS * RET_DK), lambda b, c: (b, c, 0)),
            pl.BlockSpec((nbt, ch, RET_HEADS * RET_DV), lambda b, c: (b, c, 0)),
            _const_spec(dec.shape), _const_spec(xi.shape), _const_spec(zeta.shape), _const_spec(gch.shape),
            _const_spec((RET_HEADS, 1, RET_DV)),
        ],
        out_specs=[
            pl.BlockSpec((nbt, ch, RET_HEADS * RET_DV), lambda b, c: (b, c, 0)),
            pl.BlockSpec((nbt, RET_HEADS, RET_DK, RET_DV), lambda b, c: (b, 0, 0, 0)),
        ],
        out_shape=[
            jax.ShapeDtypeStruct((bx, lx, RET_HEADS * RET_DV), BF16),
            jax.ShapeDtypeStruct((bx, RET_HEADS, RET_DK, RET_DV), F32),
        ],
        scratch_shapes=[pltpu.VMEM((nbt, RET_HEADS, RET_DK, RET_DV), F32)],
        compiler_params=_cparams("arbitrary", "arbitrary"),
        name="retention_prompt",
    )(rq, rk, rv, dec, xi, zeta, gch, gn_g.reshape(RET_HEADS, 1, RET_DV))


def _ret_sample_kernel(qt_ref, kt_ref, v_ref, s_ref, dec_ref, xi_ref, zeta_ref, gch_ref, gn_ref, *rest, bb):
    prev_refs, (o_ref, sout_ref) = rest[:-2], rest[-2:]
    for li, prev_ref in enumerate(prev_refs):
        sout_ref[li] = prev_ref[...]
    own = sout_ref.at[len(prev_refs)] if prev_refs else sout_ref
    for j in range(bb):
        for hh in range(RET_HEADS):
            qcol = qt_ref[hh * RET_DK:(hh + 1) * RET_DK, j:j + 1].astype(F32)
            kcol = kt_ref[hh * RET_DK:(hh + 1) * RET_DK, j:j + 1].astype(F32)
            vrow = v_ref[j:j + 1, hh * RET_DV:(hh + 1) * RET_DV].astype(F32)
            s0 = s_ref[j, hh]
            score = jnp.sum(qcol * kcol, axis=0, keepdims=True) * dec_ref[hh]
            inner = score * vrow
            cross = jnp.sum(qcol * s0, axis=0, keepdims=True) * xi_ref[hh]
            own[j, hh] = gch_ref[hh] * s0 + (kcol * zeta_ref[hh]) * vrow
            o_ref[j:j + 1, hh * RET_DV:(hh + 1) * RET_DV] = _group_norm(inner + cross, gn_ref[hh]).astype(BF16)


def _ret_sample_call(rq, rk, rv, state_all, prev_new, layer, ret_tabs, gn_g, bb):
    n = rq.shape[0]
    dec, xi, zeta, gch = ret_tabs
    dqk = RET_HEADS * RET_DK
    qt = rq.reshape(n // bb, bb, dqk).transpose(0, 2, 1)
    kt = rk.reshape(n // bb, bb, dqk).transpose(0, 2, 1)
    st_blk = (bb, RET_HEADS, RET_DK, RET_DV)
    if prev_new:
        n_out = len(prev_new) + 1
        st_spec = pl.BlockSpec((n_out,) + st_blk, lambda i: (0, i, 0, 0, 0))
        st_shape = jax.ShapeDtypeStruct((n_out,) + state_all.shape[1:], F32)
    else:
        st_spec = pl.BlockSpec(st_blk, lambda i: (i, 0, 0, 0))
        st_shape = jax.ShapeDtypeStruct(state_all.shape[1:], F32)
    return pl.pallas_call(
        functools.partial(_ret_sample_kernel, bb=bb),
        grid=(n // bb,),
        in_specs=[
            pl.BlockSpec((None, dqk, bb), lambda i: (i, 0, 0)),
            pl.BlockSpec((None, dqk, bb), lambda i: (i, 0, 0)),
            pl.BlockSpec((bb, RET_HEADS * RET_DV), lambda i: (i, 0)),
            pl.BlockSpec((None,) + st_blk, lambda i: (layer, i, 0, 0, 0)),
            _const_spec(dec.shape), _const_spec(xi.shape), _const_spec(zeta.shape), _const_spec(gch.shape),
            _const_spec((RET_HEADS, 1, RET_DV)),
        ] + [pl.BlockSpec(st_blk, lambda i: (i, 0, 0, 0)) for _ in prev_new],
        out_specs=[pl.BlockSpec((bb, RET_HEADS * RET_DV), lambda i: (i, 0)), st_spec],
        out_shape=[jax.ShapeDtypeStruct((n, RET_HEADS * RET_DV), BF16), st_shape],
        compiler_params=_cparams("arbitrary"),
        name="retention_sample",
    )(qt, kt, rv, state_all, dec, xi, zeta, gch, gn_g.reshape(RET_HEADS, 1, RET_DV), *prev_new)


HEADS_PER_GROUP = 4


def _lane_tile(x, n):
    return x if n == 1 else jnp.concatenate([x] * n, axis=1)


def _attn_prompt_kernel(q_ref, k_ref, v_ref, o_ref, m_scr, l_scr, acc_scr, *, tq):
    i = pl.program_id(2)
    m_scr[...] = jnp.full_like(m_scr, NEG_INF)
    l_scr[...] = jnp.zeros_like(l_scr)
    acc_scr[...] = jnp.zeros_like(acc_scr)
    srep = tq // LANES
    vrep = acc_scr.shape[-1] // LANES

    def tile(kt, bias):
        ks = pl.multiple_of(kt * tq, tq)
        vblk = v_ref[pl.ds(ks, tq), :]
        for hh in range(HEADS_PER_GROUP):
            q = q_ref[:, hh * HEAD_SLOT:(hh + 1) * HEAD_SLOT]
            k = k_ref[pl.ds(ks, tq), hh * HEAD_SLOT:(hh + 1) * HEAD_SLOT]
            s = _dot_nt(q, k)
            if bias is not None:
                s = s + bias
            m_old = m_scr[hh]
            m_new = jnp.maximum(m_old, jnp.max(s, axis=-1, keepdims=True))
            alpha = jnp.exp2(m_old - m_new)
            p = jnp.exp2(s - _lane_tile(m_new, srep))
            l_scr[hh] = alpha * l_scr[hh] + jnp.sum(p, axis=-1, keepdims=True)
            acc_scr[hh] = _lane_tile(alpha, vrep) * acc_scr[hh] + _dot(p.astype(BF16), vblk)
            m_scr[hh] = m_new

    def body(kt, carry):
        tile(kt, None)
        return carry

    lax.fori_loop(0, i, body, 0)
    row = lax.broadcasted_iota(jnp.int32, (tq, tq), 0)
    col = lax.broadcasted_iota(jnp.int32, (tq, tq), 1)
    tile(i, jnp.where(row >= col, 0.0, NEG_INF).astype(F32))

    lane = lax.broadcasted_iota(jnp.int32, o_ref.shape, 1)
    o = jnp.zeros(o_ref.shape, F32)
    for hh in range(HEADS_PER_GROUP):
        oh = acc_scr[hh] * _lane_tile(1.0 / l_scr[hh], vrep)
        o = jnp.where((lane >= hh * V_DIM) & (lane < (hh + 1) * V_DIM), oh, o)
    o_ref[...] = o.astype(BF16)


def _attn_prompt_call(q, k, v, tq):
    bx, lx, _ = q.shape
    gw = HEADS_PER_GROUP * HEAD_SLOT
    vw = HEADS_PER_GROUP * V_DIM
    ng = MLA_HEADS // HEADS_PER_GROUP
    return pl.pallas_call(
        functools.partial(_attn_prompt_kernel, tq=tq),
        grid=(bx, ng, lx // tq),
        in_specs=[
            pl.BlockSpec((None, tq, gw), lambda b, g, i: (b, i, g)),
            pl.BlockSpec((None, lx, gw), lambda b, g, i: (b, 0, g)),
            pl.BlockSpec((None, lx, vw), lambda b, g, i: (b, 0, g)),
        ],
        out_specs=pl.BlockSpec((None, tq, vw), lambda b, g, i: (b, i, g)),
        out_shape=jax.ShapeDtypeStruct((bx, lx, MLA_HEADS * V_DIM), BF16),
        scratch_shapes=[
            pltpu.VMEM((HEADS_PER_GROUP, tq, LANES), F32),
            pltpu.VMEM((HEADS_PER_GROUP, tq, LANES), F32),
            pltpu.VMEM((HEADS_PER_GROUP, tq, vw), F32),
        ],
        compiler_params=_cparams("arbitrary", "arbitrary", "arbitrary"),
        name="attention_prompt",
    )(q, k, v)


def _qlat_kernel(q_ref, w_ref, o_ref):
    o_ref[...] = _dot(q_ref[...], w_ref[...]).astype(BF16)


def _qlat_call(q2, wukt):
    n = q2.shape[0]
    r = wukt.shape[2]
    return pl.pallas_call(
        _qlat_kernel,
        grid=(MLA_HEADS,),
        in_specs=[
            pl.BlockSpec((n, HEAD_SLOT), lambda h: (0, h)),
            pl.BlockSpec((None, HEAD_SLOT, r), lambda h: (h, 0, 0)),
        ],
        out_specs=pl.BlockSpec((n, r), lambda h: (0, h)),
        out_shape=jax.ShapeDtypeStruct((n, MLA_HEADS * r), BF16),
        compiler_params=_cparams("arbitrary"),
        name="absorb_q",
    )(q2, wukt)


def _attn_paged_kernel(pt_ref, qlat_ref, q_ref, cnew_ref, pnew_ref, snew_ref, ckv_hbm, kpe_hbm, ks_hbm, o_ref,
                       ckv_buf, kpe_buf, ks_buf, sem, *, layer, cp, n_chunk):
    n_slots = ckv_buf.shape[0]
    b = pl.program_id(0)
    nb = pl.num_programs(0)

    def page_copies(bi, ci, slot, g):
        page = pt_ref[bi, ci * cp + g]
        lanes = pl.ds(pl.multiple_of(g * PAGE_SIZE, PAGE_SIZE), PAGE_SIZE)
        return (pltpu.make_async_copy(ckv_hbm.at[layer, page], ckv_buf.at[slot, g], sem.at[slot, 0]),
                pltpu.make_async_copy(kpe_hbm.at[layer, page], kpe_buf.at[slot, :, lanes], sem.at[slot, 1]),
                pltpu.make_async_copy(ks_hbm.at[layer, page], ks_buf.at[slot, :, lanes], sem.at[slot, 2]))

    def start(bi, ci, slot):
        def one(g, carry):
            for c in page_copies(bi, ci, slot, g):
                c.start()
            return carry
        lax.fori_loop(0, cp, one, 0, unroll=4)

    def wait(slot):
        for j, buf in enumerate((ckv_buf, kpe_buf, ks_buf)):
            pltpu.make_async_copy(buf.at[slot], buf.at[slot], sem.at[slot, j]).wait()

    @pl.when(b == 0)
    def _():
        for ci in range(n_chunk):
            start(0, ci, ci)

    qlat = qlat_ref[...]
    qpe = q_ref[:, NOPE_DIM:QK_DIM]
    h, r = qlat.shape
    m = jnp.full((h, 1), NEG_INF, F32)
    l = jnp.zeros((h, 1), F32)
    acc = jnp.zeros((h, r), F32)

    for ci in range(n_chunk):
        slot = lax.rem(b * n_chunk + ci, n_slots)
        wait(slot)
        ckv_bf = ckv_buf[slot].reshape(cp * PAGE_SIZE, r).astype(BF16)
        s = (_dot_nt(qlat, ckv_bf) + _dot(qpe, kpe_buf[slot].astype(BF16))) * ks_buf[slot]
        m_new = jnp.maximum(m, jnp.max(s, axis=-1, keepdims=True))
        alpha = jnp.exp2(m - m_new)
        p = jnp.exp2(s - m_new)
        l = alpha * l + jnp.sum(p, axis=-1, keepdims=True)
        acc = alpha * acc + _dot(p.astype(BF16), ckv_bf)
        m = m_new

        @pl.when(b + 1 < nb)
        def _():
            start(b + 1, ci, lax.rem((b + 1) * n_chunk + ci, n_slots))

    cn = cnew_ref[...]
    s_new = (jnp.sum(qlat.astype(F32) * cn, axis=-1, keepdims=True)
             + jnp.sum(qpe.astype(F32) * pnew_ref[...], axis=-1, keepdims=True)) * snew_ref[...]
    m_new = jnp.maximum(m, s_new)
    alpha = jnp.exp2(m - m_new)
    p_new = jnp.exp2(s_new - m_new)
    l = alpha * l + p_new
    acc = alpha * acc + p_new * cn
    o_ref[...] = (acc / l).astype(BF16)


def _attn_paged_call(page_table, qlat3, q3, cnew, pnew, snew_t, ckv_pool, kpe_pool_t, ks_pool_t, layer, cp):
    nb, n_pages = page_table.shape
    assert n_pages % cp == 0
    n_chunk = n_pages // cp
    n_slots = n_chunk + 1
    r = ckv_pool.shape[-1]
    grid_spec = pltpu.PrefetchScalarGridSpec(
        num_scalar_prefetch=1,
        grid=(nb,),
        in_specs=[
            pl.BlockSpec((None, MLA_HEADS, r), lambda b, pt: (b, 0, 0)),
            pl.BlockSpec((None, MLA_HEADS, HEAD_SLOT), lambda b, pt: (b, 0, 0)),
            pl.BlockSpec((None, 1, r), lambda b, pt: (b, 0, 0)),
            pl.BlockSpec((None, 1, ROPE_DIM), lambda b, pt: (b, 0, 0)),
            pl.BlockSpec((None, MLA_HEADS, 1), lambda b, pt: (b, 0, 0)),
            pl.BlockSpec(memory_space=pl.ANY),
            pl.BlockSpec(memory_space=pl.ANY),
            pl.BlockSpec(memory_space=pl.ANY),
        ],
        out_specs=pl.BlockSpec((None, MLA_HEADS, r), lambda b, pt: (b, 0, 0)),
        scratch_shapes=[
            pltpu.VMEM((n_slots, cp, PAGE_SIZE, r), F32),
            pltpu.VMEM((n_slots, ROPE_DIM, cp * PAGE_SIZE), F32),
            pltpu.VMEM((n_slots, MLA_HEADS, cp * PAGE_SIZE), F32),
            pltpu.SemaphoreType.DMA((n_slots, 3)),
        ],
    )
    return pl.pallas_call(
        functools.partial(_attn_paged_kernel, layer=layer, cp=cp, n_chunk=n_chunk),
        grid_spec=grid_spec,
        out_shape=jax.ShapeDtypeStruct((nb, MLA_HEADS, r), BF16),
        compiler_params=_cparams("arbitrary"),
        name="attention_paged",
    )(page_table, qlat3, q3, cnew, pnew, snew_t, ckv_pool, kpe_pool_t, ks_pool_t)


def _uv_pair_kernel(o_ref, w_ref, y_ref):
    r = w_ref.shape[1]
    parts = [_dot(o_ref[:, j * r:(j + 1) * r], w_ref[j]) for j in range(w_ref.shape[0])]
    y_ref[...] = jnp.concatenate(parts, axis=-1).astype(BF16)


def _uv_call(olat2, wuv_h):
    n = olat2.shape[0]
    r = wuv_h.shape[1]
    hpb = LANES // V_DIM
    return pl.pallas_call(
        _uv_pair_kernel,
        grid=(MLA_HEADS // hpb,),
        in_specs=[
            pl.BlockSpec((n, hpb * r), lambda h: (0, h)),
            pl.BlockSpec((hpb, r, V_DIM), lambda h: (h, 0, 0)),
        ],
        out_specs=pl.BlockSpec((n, hpb * V_DIM), lambda h: (0, h)),
        out_shape=jax.ShapeDtypeStruct((n, MLA_HEADS * V_DIM), BF16),
        compiler_params=_cparams("arbitrary"),
        name="value_up",
    )(olat2, wuv_h)


def _outproj_kernel(x_ref, sh_ref, sc_ref, gt_ref, g_ref, wg_ref, ret_ref, att_ref, wpa_ref, wpb_ref, wo_ref, o_ref):
    x = x_ref[...]
    d = x.shape[-1]
    h = _modulated_norm(x, g_ref[...], sh_ref[...], sc_ref[...]).astype(BF16)
    gates = _dot(h, wg_ref[...])
    nv = ret_ref.shape[-1]
    rg, ga, gb = gates[:, :nv], gates[:, nv:nv + d], gates[:, nv + d:]
    ya = _dot((ret_ref[...].astype(F32) * (rg * jax.nn.sigmoid(rg))).astype(BF16), wpa_ref[...])
    yb = _dot(att_ref[...], wpb_ref[...])
    merged = jax.nn.sigmoid(ga) * ya + jax.nn.sigmoid(gb) * yb
    o_ref[...] = x + gt_ref[...] * _dot(merged.astype(BF16), wo_ref[...])


def _outproj_call(x3, mod3, norm_g, w_gates, ret_o, att_o, wpa, wpb, wo, tl):
    bx, lx, d = x3.shape
    lm = mod3.shape[1]
    lm_blk = 1 if lm == 1 else tl

    def tok(n):
        return pl.BlockSpec((None, tl, n), lambda b, i: (b, i, 0))

    return pl.pallas_call(
        _outproj_kernel,
        grid=(bx, lx // tl),
        in_specs=[
            tok(d), _mod_spec(lm_blk, d, 3), _mod_spec(lm_blk, d, 4), _mod_spec(lm_blk, d, 5), _const_spec((1, d)),
            _const_spec(w_gates.shape), tok(ret_o.shape[-1]), tok(att_o.shape[-1]),
            _const_spec(wpa.shape), _const_spec(wpb.shape), _const_spec(wo.shape),
        ],
        out_specs=tok(d),
        out_shape=jax.ShapeDtypeStruct(x3.shape, F32),
        compiler_params=_cparams("arbitrary", "arbitrary"),
        name="mixer_outproj",
    )(x3, mod3, mod3, mod3, norm_g.reshape(1, d), w_gates, ret_o, att_o, wpa, wpb, wo)


def _rope_tables(pos):
    half_r = RET_DK // 2
    fr = ROPE_THETA ** (-jnp.arange(half_r, dtype=F32) / half_r)
    ang = pos[:, None] * fr[None, :]
    cos, sin = jnp.cos(ang), jnp.sin(ang)
    cosr = jnp.concatenate([cos, cos], axis=-1)
    sinr = jnp.concatenate([-sin, sin], axis=-1)
    half_m = ROPE_DIM // 2
    fm = ROPE_THETA ** (-jnp.arange(half_m, dtype=F32) / half_m)
    angm = pos[:, None] * fm[None, :]
    cm, sm = jnp.cos(angm), jnp.sin(angm)
    n = pos.shape[0]
    z16, z32 = jnp.zeros((n, half_m), F32), jnp.zeros((n, LANES - QK_DIM), F32)
    rc = jnp.concatenate([jnp.ones((n, NOPE_DIM), F32), cm, cm, z32], axis=-1)
    rsm = jnp.concatenate([jnp.zeros((n, NOPE_DIM), F32), -sm, z16, z32], axis=-1)
    rsp = jnp.concatenate([jnp.zeros((n, NOPE_DIM), F32), z16, sm, z32], axis=-1)
    return cosr, sinr, rc, rsm, rsp


def _retention_tables(chunk):
    log_gamma = jnp.log1p(-jnp.exp2(-5.0 - jnp.arange(RET_HEADS, dtype=F32)))
    idx = jnp.arange(chunk, dtype=F32)
    rel = idx[:, None] - idx[None, :]
    decay = jnp.where(rel[None] >= 0, jnp.exp(rel[None] * log_gamma[:, None, None]), 0.0)
    xi = jnp.exp((idx[None, :] + 1.0) * log_gamma[:, None])[:, :, None]
    zeta = jnp.exp((chunk - 1.0 - idx)[None, :] * log_gamma[:, None])[:, :, None]
    gch = jnp.exp(chunk * log_gamma)[:, None, None]
    return decay, xi, zeta, gch


def _pad_heads(w, width):
    r, hn, n = w.shape
    return jnp.pad(w, ((0, 0), (0, 0), (0, width - n))).reshape(r, hn * width)


def _prep_layer(l, w_in, w_uq, w_uk, w_uv, w_pa, w_pb, w_out, mla_qa_g, mla_kva_g, mla_qn_g, mla_kn_g):
    d = w_in.shape[1]
    rqk, rvw = RET_HEADS * RET_DK, RET_HEADS * RET_DV
    q_rank, kv_rank = w_uq.shape[1], w_uk.shape[1]
    o = np.cumsum([0, rqk, rqk, rvw, rvw, q_rank, kv_rank, ROPE_DIM, d, d])
    wi = w_in[l]
    rq, rk, rv, rg, cq, ckv, kpe, ga, gb = [wi[:, o[i]:o[i + 1]] for i in range(9)]
    kpe_slot = jnp.pad(kpe, ((0, 0), (NOPE_DIM, LANES - QK_DIM)))
    w_a = jnp.concatenate([rv, rq, rk, cq, ckv, kpe_slot], axis=1).astype(BF16)
    offs = tuple(int(v) for v in np.cumsum([0, rvw, rqk, rqk, q_rank, kv_rank, LANES]))
    w_g = jnp.concatenate([rg, ga, gb], axis=1).astype(BF16)
    wuq = _pad_heads(w_uq[l], HEAD_SLOT).astype(BF16)
    wuk = _pad_heads(w_uk[l], HEAD_SLOT).astype(BF16)
    wuv = w_uv[l].reshape(kv_rank, MLA_HEADS * V_DIM).astype(BF16)
    wukt = jnp.pad(jnp.transpose(w_uk[l], (1, 2, 0)), ((0, 0), (0, HEAD_SLOT - NOPE_DIM), (0, 0))).astype(BF16)
    wuv_h = jnp.transpose(w_uv[l], (1, 0, 2)).astype(BF16)
    gqn, gkn = mla_qn_g[l], mla_kn_g[l]
    zpad = jnp.zeros((LANES - QK_DIM,), F32)
    aq = (jnp.concatenate([gqn[:NOPE_DIM] * gkn[:NOPE_DIM], gqn[NOPE_DIM:], zpad]) * (MLA_SCALE * LOG2E)).reshape(1, LANES)
    ak = jnp.concatenate([jnp.zeros((NOPE_DIM,), F32), gkn[NOPE_DIM:], zpad]).reshape(1, LANES)
    gains = (mla_qa_g[l].reshape(1, q_rank), mla_kva_g[l].reshape(1, kv_rank), aq, ak)
    return dict(w_a=w_a, offs=offs, w_g=w_g, wuq=wuq, wuk=wuk, wuv=wuv, wukt=wukt, wuv_h=wuv_h, gains=gains,
                wpa=w_pa[l].astype(BF16), wpb=w_pb[l].astype(BF16), wo=w_out[l].astype(BF16))


def _largest_tile(n, cap):
    t = min(n, cap)
    while n % t:
        t //= 2
    return t


def kernel(x_prompt, x_sample, cache_ckv, cache_kpe, cache_kscale, state_ret, page_table, c_prompt, c_sample, ada_w, ada_b, norm_g, ffn_w13, ffn_w2, w_in, ret_gn_g, mla_qa_g, mla_kva_g, w_uq, mla_qn_g, mla_kn_g, w_uk, w_uv, w_pa, w_pb, w_out):
    depth = ada_w.shape[0]
    bp, lp, d = x_prompt.shape
    ns, ls, _ = x_sample.shape
    assert ls == 1, "sample group is one new token per sequence"
    assert lp % RET_CHUNK == 0
    n_pages = page_table.shape[1]
    past_len = n_pages * PAGE_SIZE

    mod = _mod_call(jnp.concatenate([c_prompt, c_sample], axis=0), ada_w, ada_b)
    tabs_p = _rope_tables(jnp.arange(lp, dtype=F32))
    tabs_s = _rope_tables(jnp.full((ns,), float(past_len), F32))
    rt_p = _retention_tables(RET_CHUNK)
    rt_s = _retention_tables(1)

    tl_p = _largest_tile(lp, 512)
    tl_in = _largest_tile(lp, 256)
    tq = _largest_tile(lp, 512)
    bb = _largest_tile(ns, 8)
    nbt = _largest_tile(bp, 2)
    cp = _largest_tile(n_pages // 2, 64)
    kpe_pool_t = jnp.swapaxes(cache_kpe, 2, 3)
    ks_pool_t = jnp.swapaxes(cache_kscale, 2, 3)

    w13_all = ffn_w13.astype(BF16)
    w2_all = ffn_w2.astype(BF16)

    yp = x_prompt
    ys = x_sample.reshape(1, ns, d)
    st_p, st_s = [], []
    ret_s_new = []
    for l in range(depth):
        w = _prep_layer(l, w_in, w_uq, w_uk, w_uv, w_pa, w_pb, w_out, mla_qa_g, mla_kva_g, mla_qn_g, mla_kn_g)
        mod_p = mod[l, :bp].reshape(bp, 1, N_MOD * d)
        mod_s = mod[l, bp:].reshape(1, ns, N_MOD * d)
        ng = norm_g[l]

        yp = _ffn_call(yp, mod_p, 0, ng[0], w13_all, w2_all, l, 0, tl_p)
        rq, rk, rv, q, ckv, kpe, ks, kf, vf = _inproj_call(
            yp, mod_p, ng[1], w["w_a"], w["offs"], tabs_p, w["gains"], w["wuq"], w["wuk"], w["wuv"], tl_in, True)
        ret_o, ret_s = _ret_prompt_call(rq, rk, rv, rt_p, ret_gn_g[l], nbt)
        att_o = _attn_prompt_call(q, kf, vf, tq)
        yp = _outproj_call(yp, mod_p, ng[1], w["w_g"], ret_o, att_o, w["wpa"], w["wpb"], w["wo"], tl_p)
        yp = _ffn_call(yp, mod_p, 6, ng[2], w13_all, w2_all, l, 1, tl_p)
        st_p.append((ckv, kpe, ks, ret_s))

        ys = _ffn_call(ys, mod_s, 0, ng[0], w13_all, w2_all, l, 0, ns)
        rq, rk, rv, q, ckv, kpe, ks = _inproj_call(
            ys, mod_s, ng[1], w["w_a"], w["offs"], tabs_s, w["gains"], w["wuq"], w["wuk"], w["wuv"], ns, False)
        last = l == depth - 1
        ret_o, ret_s = _ret_sample_call(rq[0], rk[0], rv[0], state_ret, ret_s_new if last else [], l, rt_s,
                                        ret_gn_g[l], bb)
        ret_s_new.append(ret_s)
        qlat = _qlat_call(q[0], w["wukt"])

        olat = _attn_paged_call(page_table, qlat.reshape(ns, MLA_HEADS, -1), q[0].reshape(ns, MLA_HEADS, HEAD_SLOT),
                                ckv.reshape(ns, 1, -1), kpe.reshape(ns, 1, -1), ks.reshape(ns, MLA_HEADS, 1),
                                cache_ckv, kpe_pool_t, ks_pool_t, l, cp)
        att_o = _uv_call(olat.reshape(ns, -1), w["wuv_h"])
        ys = _outproj_call(ys, mod_s, ng[1], w["w_g"], ret_o[None], att_o[None], w["wpa"], w["wpb"], w["wo"], ns)
        ys = _ffn_call(ys, mod_s, 6, ng[2], w13_all, w2_all, l, 1, ns)
        st_s.append((ckv.reshape(ns, 1, -1), kpe.reshape(ns, 1, -1), ks.reshape(ns, 1, -1)))

    def stack(states, i):
        return jnp.stack([st[i] for st in states], axis=0)

    return (yp, ys.reshape(ns, 1, d),
            stack(st_p, 0), stack(st_p, 1), stack(st_p, 2), stack(st_p, 3),
            stack(st_s, 0), stack(st_s, 1), stack(st_s, 2),
            ret_s_new[-1] if depth > 1 else ret_s_new[-1][None])
```

```python
import functools

import jax
import jax.numpy as jnp
import numpy as np
from jax import lax
from jax.experimental import pallas as pl
from jax.experimental.pallas import tpu as pltpu

F32 = jnp.float32
BF16 = jnp.bfloat16

RET_HEADS = 4
RET_DK = 128
RET_DV = 256
RET_CHUNK = 128
MLA_HEADS = 16
NOPE_DIM = 64
ROPE_DIM = 32
V_DIM = 64
QK_DIM = NOPE_DIM + ROPE_DIM
MLA_SCALE = QK_DIM ** -0.5
LOG2E = float(np.log2(np.e))
ROPE_THETA = 10000.0
N_MOD = 9
EPS = 1e-6
PAGE_SIZE = 128

LANES = 128
MXU_DEPTH = 256
HEAD_SLOT = LANES
VMEM_LIMIT = 56 * 1024 * 1024
NEG_INF = float("-inf")


def _cparams(*sem):
    return pltpu.CompilerParams(dimension_semantics=sem, vmem_limit_bytes=VMEM_LIMIT)


def _dot(a, b):
    return jnp.dot(a, b, preferred_element_type=F32)


def _dot_nt(a, b):
    return lax.dot_general(a, b, (((1,), (1,)), ((), ())), preferred_element_type=F32)


def _dot_tn(a, b):
    return lax.dot_general(a, b, (((0,), (0,)), ((), ())), preferred_element_type=F32)


def _const_spec(shape):
    n = len(shape)
    return pl.BlockSpec(shape, lambda *_: (0,) * n)


def _modulated_norm(x, g, shift, scale):
    ms = jnp.mean(x * x, axis=-1, keepdims=True)
    y = x * lax.rsqrt(ms + EPS) * g
    return y * (1.0 + scale) + shift


def _mod_spec(lm_blk, d, k):
    if lm_blk == 1:
        return pl.BlockSpec((None, 1, d), lambda b, i: (b, 0, k))
    return pl.BlockSpec((None, lm_blk, d), lambda b, i: (b, i, k))


def _mod_kernel(c_ref, w_ref, b_ref, o_ref):
    c = c_ref[...]
    s = (c * jax.nn.sigmoid(c)).astype(BF16)
    o_ref[...] = _dot(s, w_ref[...].astype(BF16)) + b_ref[...]


def _mod_call(c_all, ada_w, ada_b):
    depth, d, nd = ada_w.shape
    n = c_all.shape[0]
    return pl.pallas_call(
        _mod_kernel,
        grid=(depth, nd // d),
        in_specs=[
            pl.BlockSpec((n, d), lambda l, j: (0, 0)),
            pl.BlockSpec((None, d, d), lambda l, j: (l, 0, j)),
            pl.BlockSpec((None, 1, d), lambda l, j: (l, 0, j)),
        ],
        out_specs=pl.BlockSpec((None, n, d), lambda l, j: (l, 0, j)),
        out_shape=jax.ShapeDtypeStruct((depth, n, nd), F32),
        compiler_params=_cparams("arbitrary", "arbitrary"),
        name="adaln_mod",
    )(c_all, ada_w, ada_b.reshape(depth, 1, nd))


def _ffn_kernel(x_ref, sh_ref, sc_ref, gt_ref, g_ref, w13_ref, w2_ref, o_ref, *, d_ff):
    x = x_ref[...]
    h = _modulated_norm(x, g_ref[...], sh_ref[...], sc_ref[...]).astype(BF16)
    ab = _dot(h, w13_ref[...])
    a, b = ab[:, :d_ff], ab[:, d_ff:]
    u = (a * jax.nn.sigmoid(a) * b).astype(BF16)
    y = _dot(u, w2_ref[...])
    o_ref[...] = x + 0.5 * gt_ref[...] * y


def _ffn_call(x3, mod3, k0, norm_g, w13_all, w2_all, layer, j, tl):
    bx, lx, d = x3.shape
    lm = mod3.shape[1]
    lm_blk = 1 if lm == 1 else tl
    d_ff = w2_all.shape[2]
    return pl.pallas_call(
        functools.partial(_ffn_kernel, d_ff=d_ff),
        grid=(bx, lx // tl),
        in_specs=[
            pl.BlockSpec((None, tl, d), lambda b, i: (b, i, 0)),
            _mod_spec(lm_blk, d, k0), _mod_spec(lm_blk, d, k0 + 1), _mod_spec(lm_blk, d, k0 + 2),
            _const_spec((1, d)),
            pl.BlockSpec((None, None) + w13_all.shape[2:], lambda b, i: (layer, j, 0, 0), pipeline_mode=pl.Buffered(1)),
            pl.BlockSpec((None, None) + w2_all.shape[2:], lambda b, i: (layer, j, 0, 0), pipeline_mode=pl.Buffered(1)),
        ],
        out_specs=pl.BlockSpec((None, tl, d), lambda b, i: (b, i, 0)),
        out_shape=jax.ShapeDtypeStruct(x3.shape, F32),
        compiler_params=_cparams("arbitrary", "arbitrary"),
        name="ffn",
    )(x3, mod3, mod3, mod3, norm_g.reshape(1, d), w13_all, w2_all)


def _rope_slot(t, c, sm, sp):
    return t * c + pltpu.roll(t, LANES - ROPE_DIM // 2, axis=1) * sm + pltpu.roll(t, ROPE_DIM // 2, axis=1) * sp


def _slot_sumsq(v, ones_ref):
    w = ones_ref.shape[0]
    ones_bd = ones_ref[...]
    return jnp.concatenate([_dot((v[:, j:j + w] * v[:, j:j + w]).astype(BF16), ones_bd)
                            for j in range(0, v.shape[1], w)], axis=1)


def _inproj_kernel(x_ref, sh_ref, sc_ref, g_ref, w_ref, cosr_ref, sinr_ref, rc_ref, rsm_ref, rsp_ref,
                   gqa_ref, gkva_ref, aq_ref, ak_ref, wuq_ref, wuk_ref, wuv_ref, ones_ref,
                   rq_ref, rk_ref, rv_ref, q_ref, ckv_ref, kpe_ref, ks_ref, *kv_refs, offs):
    _inproj_rows(slice(0, x_ref.shape[0]), x_ref, sh_ref, sc_ref, g_ref, w_ref, cosr_ref, sinr_ref,
                 rc_ref, rsm_ref, rsp_ref, gqa_ref, gkva_ref, aq_ref, ak_ref, wuq_ref, wuk_ref, wuv_ref, ones_ref,
                 rq_ref, rk_ref, rv_ref, q_ref, ckv_ref, kpe_ref, ks_ref, kv_refs, offs)


def _inproj_rows(rows, x_ref, sh_ref, sc_ref, g_ref, w_ref, cosr_ref, sinr_ref, rc_ref, rsm_ref, rsp_ref,
                 gqa_ref, gkva_ref, aq_ref, ak_ref, wuq_ref, wuk_ref, wuv_ref, ones_ref,
                 rq_ref, rk_ref, rv_ref, q_ref, ckv_ref, kpe_ref, ks_ref, kv_refs, offs):
    o_rv, o_rq, o_rk, o_cq, o_ckv, o_kpe, n_in = offs

    def mod_rows(ref):
        return ref[...] if ref.shape[0] == 1 else ref[rows, :]

    x = x_ref[rows, :]
    h = _modulated_norm(x, g_ref[...], mod_rows(sh_ref), mod_rows(sc_ref)).astype(BF16)
    p = _dot(h, w_ref[...])

    rv_ref[rows, :] = p[:, o_rv:o_rq].astype(BF16)
    cosr, sinr = cosr_ref[rows, :], sinr_ref[rows, :]
    for hh in range(RET_HEADS):
        lo, hi = hh * RET_DK, (hh + 1) * RET_DK
        tq = p[:, o_rq + lo:o_rq + hi]
        rq_ref[rows, lo:hi] = (tq * cosr + pltpu.roll(tq, RET_DK // 2, axis=1) * sinr).astype(BF16)
        tk = p[:, o_rk + lo:o_rk + hi]
        rk_ref[rows, lo:hi] = ((tk * cosr + pltpu.roll(tk, RET_DK // 2, axis=1) * sinr) * (RET_DK ** -0.5)).astype(BF16)

    rc, rsm, rsp = rc_ref[rows, :], rsm_ref[rows, :], rsp_ref[rows, :]

    cq = p[:, o_cq:o_ckv]
    cq = (cq * lax.rsqrt(jnp.mean(cq * cq, axis=-1, keepdims=True) + EPS) * gqa_ref[...]).astype(BF16)
    qf = _dot(cq, wuq_ref[...])
    qn = qf * lax.rsqrt(_slot_sumsq(qf, ones_ref) * (1.0 / QK_DIM) + EPS) * aq_ref[...]
    for hh in range(MLA_HEADS):
        lo, hi = hh * HEAD_SLOT, (hh + 1) * HEAD_SLOT
        q_ref[rows, lo:hi] = _rope_slot(qn[:, lo:hi], rc, rsm, rsp).astype(BF16)

    craw = p[:, o_ckv:o_kpe]
    ckv = craw * lax.rsqrt(jnp.mean(craw * craw, axis=-1, keepdims=True) + EPS) * gkva_ref[...]
    ckv_ref[rows, :] = ckv
    ckv_bf = ckv.astype(BF16)
    kn = _dot(ckv_bf, wuk_ref[...])
    kblk = p[:, o_kpe:n_in]
    kpe_ss = _dot((kblk * kblk).astype(BF16), ones_ref[:HEAD_SLOT, :HEAD_SLOT])
    kpe_rot = _rope_slot(kblk * ak_ref[...], rc, rsm, rsp)
    kpe_ref[rows, :] = kpe_rot[:, NOPE_DIM:QK_DIM]
    ms = (_slot_sumsq(kn, ones_ref) + _lane_tile(kpe_ss, MLA_HEADS)) * (1.0 / QK_DIM)
    ks_rep = lax.rsqrt(ms + EPS)
    lane16 = lax.broadcasted_iota(jnp.int32, (x.shape[0], MLA_HEADS), 1)
    ks_all = jnp.zeros((x.shape[0], MLA_HEADS), F32)
    for hh in range(MLA_HEADS):
        ks_all = jnp.where(lane16 == hh, ks_rep[:, hh * HEAD_SLOT:hh * HEAD_SLOT + MLA_HEADS], ks_all)
    ks_ref[rows, :] = ks_all
    if kv_refs:
        kv_refs[0][rows, :] = ((kn + _lane_tile(kpe_rot, MLA_HEADS)) * ks_rep).astype(BF16)
        kv_refs[1][rows, :] = _dot(ckv_bf, wuv_ref[...]).astype(BF16)


def _inproj_call(x3, mod3, norm_g, w_in_a, offs, tabs, gains, wuq, wuk, wuv, tl, emit_kv):
    bx, lx, d = x3.shape
    lm = mod3.shape[1]
    lm_blk = 1 if lm == 1 else tl
    cosr, sinr, rc, rsm, rsp = tabs
    gqa, gkva, aq, ak = gains
    hs = MLA_HEADS * HEAD_SLOT
    aq = jnp.tile(aq, (1, MLA_HEADS))
    slot_id = jnp.arange(MXU_DEPTH) // HEAD_SLOT
    ones_bd = (slot_id[:, None] == slot_id[None, :]).astype(BF16)

    def tok(n):
        return pl.BlockSpec((None, tl, n), lambda b, i: (b, i, 0))

    def tab():
        return pl.BlockSpec((tl, LANES), lambda b, i: (i, 0))

    def out(n, dt):
        return jax.ShapeDtypeStruct((bx, lx, n), dt)

    out_specs = [tok(RET_HEADS * RET_DK), tok(RET_HEADS * RET_DK), tok(RET_HEADS * RET_DV), tok(hs),
                 tok(wuk.shape[0]), tok(ROPE_DIM), tok(MLA_HEADS)]
    out_shape = [out(RET_HEADS * RET_DK, BF16), out(RET_HEADS * RET_DK, BF16), out(RET_HEADS * RET_DV, BF16),
                 out(hs, BF16), out(wuk.shape[0], F32), out(ROPE_DIM, F32), out(MLA_HEADS, F32)]
    if emit_kv:
        out_specs += [tok(hs), tok(MLA_HEADS * V_DIM)]
        out_shape += [out(hs, BF16), out(MLA_HEADS * V_DIM, BF16)]
    return pl.pallas_call(
        functools.partial(_inproj_kernel, offs=offs),
        grid=(bx, lx // tl),
        in_specs=[
            tok(d), _mod_spec(lm_blk, d, 3), _mod_spec(lm_blk, d, 4), _const_spec((1, d)),
            _const_spec(w_in_a.shape), tab(), tab(), tab(), tab(), tab(),
            _const_spec(gqa.shape), _const_spec(gkva.shape), _const_spec(aq.shape), _const_spec(ak.shape),
            _const_spec(wuq.shape), _const_spec(wuk.shape), _const_spec(wuv.shape), _const_spec(ones_bd.shape),
        ],
        out_specs=out_specs,
        out_shape=out_shape,
        compiler_params=_cparams("arbitrary", "arbitrary"),
        name="mixer_inproj",
    )(x3, mod3, mod3, norm_g.reshape(1, d), w_in_a, cosr, sinr, rc, rsm, rsp, gqa, gkva, aq, ak, wuq, wuk, wuv,
      ones_bd)


def _group_norm(o, g):
    mu = jnp.mean(o, axis=-1, keepdims=True)
    oc = o - mu
    var = jnp.mean(oc * oc, axis=-1, keepdims=True)
    return oc * lax.rsqrt(var + EPS) * g


def _ret_prompt_kernel(q_ref, k_ref, v_ref, dec_ref, xi_ref, zeta_ref, gch_ref, gn_ref, o_ref, sout_ref, s_scr):
    c = pl.program_id(1)

    @pl.when(c == 0)
    def _():
        s_scr[...] = jnp.zeros_like(s_scr)

    ch = dec_ref.shape[1]
    for bi in range(q_ref.shape[0]):
        for hh in range(RET_HEADS):
            s = s_scr[bi, hh]
            for ci in range(q_ref.shape[1] // ch):
                rows = slice(ci * ch, (ci + 1) * ch)
                q = q_ref[bi, rows, hh * RET_DK:(hh + 1) * RET_DK]
                k = k_ref[bi, rows, hh * RET_DK:(hh + 1) * RET_DK]
                v = v_ref[bi, rows, hh * RET_DV:(hh + 1) * RET_DV]
                scores = _dot_nt(q, k) * dec_ref[hh]
                inner = _dot(scores.astype(BF16), v)
                cross = _dot(q, s.astype(BF16)) * xi_ref[hh]
                kz = (k.astype(F32) * zeta_ref[hh]).astype(BF16)
                s = gch_ref[hh] * s + _dot_tn(kz, v)
                o_ref[bi, rows, hh * RET_DV:(hh + 1) * RET_DV] = _group_norm(inner + cross, gn_ref[hh]).astype(BF16)
            s_scr[bi, hh] = s

    @pl.when(c == pl.num_programs(1) - 1)
    def _():
        sout_ref[...] = s_scr[...]


def _ret_prompt_call(rq, rk, rv, ret_tabs, gn_g, nbt, chunks_per_step):
    bx, lx, _ = rq.shape
    ch = RET_CHUNK * chunks_per_step
    dec, xi, zeta, gch = ret_tabs
    return pl.pallas_call(
        _ret_prompt_kernel,
        grid=(bx // nbt, lx // ch),
        in_specs=[
            pl.BlockSpec((nbt, ch, RET_HEADS * RET_DK), lambda b, c: (b, c, 0)),
            pl.BlockSpec((nbt, ch, RET_HEADS * RET_DK), lambda b, c: (b, c, 0)),
            pl.BlockSpec((nbt, ch, RET_HEADS * RET_DV), lambda b, c: (b, c, 0)),
            _const_spec(dec.shape), _const_spec(xi.shape), _const_spec(zeta.shape), _const_spec(gch.shape),
            _const_spec((RET_HEADS, 1, RET_DV)),
        ],
        out_specs=[
            pl.BlockSpec((nbt, ch, RET_HEADS * RET_DV), lambda b, c: (b, c, 0)),
            pl.BlockSpec((nbt, RET_HEADS, RET_DK, RET_DV), lambda b, c: (b, 0, 0, 0)),
        ],
        out_shape=[
            jax.ShapeDtypeStruct((bx, lx, RET_HEADS * RET_DV), BF16),
            jax.ShapeDtypeStruct((bx, RET_HEADS, RET_DK, RET_DV), F32),
        ],
        scratch_shapes=[pltpu.VMEM((nbt, RET_HEADS, RET_DK, RET_DV), F32)],
        compiler_params=_cparams("arbitrary", "arbitrary"),
        name="retention_prompt",
    )(rq, rk, rv, dec, xi, zeta, gch, gn_g.reshape(RET_HEADS, 1, RET_DV))


def _ret_sample_kernel(qt_ref, kt_ref, v_ref, s_ref, dec_ref, xi_ref, zeta_ref, gch_ref, gn_ref, *rest, bb):
    prev_refs, (o_ref, sout_ref) = rest[:-2], rest[-2:]
    for li, prev_ref in enumerate(prev_refs):
        sout_ref[li] = prev_ref[...]
    own = sout_ref.at[len(prev_refs)] if prev_refs else sout_ref
    for j in range(bb):
        for hh in range(RET_HEADS):
            qcol = qt_ref[hh * RET_DK:(hh + 1) * RET_DK, j:j + 1].astype(F32)
            kcol = kt_ref[hh * RET_DK:(hh + 1) * RET_DK, j:j + 1].astype(F32)
            vrow = v_ref[j:j + 1, hh * RET_DV:(hh + 1) * RET_DV].astype(F32)
            s0 = s_ref[j, hh]
            score = jnp.sum(qcol * kcol, axis=0, keepdims=True) * dec_ref[hh]
            inner = score * vrow
            cross = jnp.sum(qcol * s0, axis=0, keepdims=True) * xi_ref[hh]
            own[j, hh] = gch_ref[hh] * s0 + (kcol * zeta_ref[hh]) * vrow
            o_ref[j:j + 1, hh * RET_DV:(hh + 1) * RET_DV] = _group_norm(inner + cross, gn_ref[hh]).astype(BF16)


def _ret_sample_call(rq, rk, rv, state_all, prev_new, layer, ret_tabs, gn_g, bb):
    n = rq.shape[0]
    dec, xi, zeta, gch = ret_tabs
    dqk = RET_HEADS * RET_DK
    qt = rq.reshape(n // bb, bb, dqk).transpose(0, 2, 1)
    kt = rk.reshape(n // bb, bb, dqk).transpose(0, 2, 1)
    st_blk = (bb, RET_HEADS, RET_DK, RET_DV)
    if prev_new:
        n_out = len(prev_new) + 1
        st_spec = pl.BlockSpec((n_out,) + st_blk, lambda i: (0, i, 0, 0, 0))
        st_shape = jax.ShapeDtypeStruct((n_out,) + state_all.shape[1:], F32)
    else:
        st_spec = pl.BlockSpec(st_blk, lambda i: (i, 0, 0, 0))
        st_shape = jax.ShapeDtypeStruct(state_all.shape[1:], F32)
    return pl.pallas_call(
        functools.partial(_ret_sample_kernel, bb=bb),
        grid=(n // bb,),
        in_specs=[
            pl.BlockSpec((None, dqk, bb), lambda i: (i, 0, 0)),
            pl.BlockSpec((None, dqk, bb), lambda i: (i, 0, 0)),
            pl.BlockSpec((bb, RET_HEADS * RET_DV), lambda i: (i, 0)),
            pl.BlockSpec((None,) + st_blk, lambda i: (layer, i, 0, 0, 0)),
            _const_spec(dec.shape), _const_spec(xi.shape), _const_spec(zeta.shape), _const_spec(gch.shape),
            _const_spec((RET_HEADS, 1, RET_DV)),
        ] + [pl.BlockSpec(st_blk, lambda i: (i, 0, 0, 0)) for _ in prev_new],
        out_specs=[pl.BlockSpec((bb, RET_HEADS * RET_DV), lambda i: (i, 0)), st_spec],
        out_shape=[jax.ShapeDtypeStruct((n, RET_HEADS * RET_DV), BF16), st_shape],
        compiler_params=_cparams("arbitrary"),
        name="retention_sample",
    )(qt, kt, rv, state_all, dec, xi, zeta, gch, gn_g.reshape(RET_HEADS, 1, RET_DV), *prev_new)


HEADS_PER_GROUP = 4


def _lane_tile(x, n):
    return x if n == 1 else jnp.concatenate([x] * n, axis=1)


def _attn_prompt_kernel(q_ref, k_ref, v_ref, o_ref, m_scr, l_scr, acc_scr, *, tq):
    i = pl.program_id(2)
    m_scr[...] = jnp.full_like(m_scr, NEG_INF)
    l_scr[...] = jnp.zeros_like(l_scr)
    acc_scr[...] = jnp.zeros_like(acc_scr)
    vrep = acc_scr.shape[-1] // LANES

    def tile(k0, rows, nk, bias):
        vblk = v_ref[pl.ds(k0, nk), :]
        logits = [_dot_nt(q_ref[rows, hh * HEAD_SLOT:(hh + 1) * HEAD_SLOT],
                          k_ref[pl.ds(k0, nk), hh * HEAD_SLOT:(hh + 1) * HEAD_SLOT])
                  for hh in range(HEADS_PER_GROUP)]
        for hh in range(HEADS_PER_GROUP):
            s = logits[hh]
            if bias is not None:
                s = s + bias
            m_old = m_scr[hh, rows, :]
            m_new = jnp.maximum(m_old, jnp.max(s, axis=-1, keepdims=True))
            alpha = jnp.exp2(m_old - m_new)
            p = jnp.exp2(s - _lane_tile(m_new, nk // LANES))
            l_scr[hh, rows, :] = alpha * l_scr[hh, rows, :] + jnp.sum(p, axis=-1, keepdims=True)
            acc_scr[hh, rows, :] = _lane_tile(alpha, vrep) * acc_scr[hh, rows, :] + _dot(p.astype(BF16), vblk)
            m_scr[hh, rows, :] = m_new

    def body(kt, carry):
        tile(pl.multiple_of(kt * tq, tq), slice(0, tq), tq, None)
        return carry

    lax.fori_loop(0, i, body, 0)

    half = tq // 2 if (tq // 2) % LANES == 0 else tq
    k0 = pl.multiple_of(i * tq, tq)
    row = lax.broadcasted_iota(jnp.int32, (half, half), 0)
    col = lax.broadcasted_iota(jnp.int32, (half, half), 1)
    tri = jnp.where(row >= col, 0.0, NEG_INF).astype(F32)
    tile(k0, slice(0, half), half, tri)
    if half < tq:
        tile(k0, slice(half, tq), tq, jnp.concatenate([jnp.zeros_like(tri), tri], axis=1))

    lane = lax.broadcasted_iota(jnp.int32, o_ref.shape, 1)
    o = jnp.zeros(o_ref.shape, F32)
    for hh in range(HEADS_PER_GROUP):
        oh = acc_scr[hh] * _lane_tile(1.0 / l_scr[hh], vrep)
        o = jnp.where((lane >= hh * V_DIM) & (lane < (hh + 1) * V_DIM), oh, o)
    o_ref[...] = o.astype(BF16)


def _attn_prompt_call(q, k, v, tq):
    bx, lx, _ = q.shape
    gw = HEADS_PER_GROUP * HEAD_SLOT
    vw = HEADS_PER_GROUP * V_DIM
    ng = MLA_HEADS // HEADS_PER_GROUP
    return pl.pallas_call(
        functools.partial(_attn_prompt_kernel, tq=tq),
        grid=(bx, ng, lx // tq),
        in_specs=[
            pl.BlockSpec((None, tq, gw), lambda b, g, i: (b, i, g)),
            pl.BlockSpec((None, lx, gw), lambda b, g, i: (b, 0, g)),
            pl.BlockSpec((None, lx, vw), lambda b, g, i: (b, 0, g)),
        ],
        out_specs=pl.BlockSpec((None, tq, vw), lambda b, g, i: (b, i, g)),
        out_shape=jax.ShapeDtypeStruct((bx, lx, MLA_HEADS * V_DIM), BF16),
        scratch_shapes=[
            pltpu.VMEM((HEADS_PER_GROUP, tq, LANES), F32),
            pltpu.VMEM((HEADS_PER_GROUP, tq, LANES), F32),
            pltpu.VMEM((HEADS_PER_GROUP, tq, vw), F32),
        ],
        compiler_params=_cparams("arbitrary", "arbitrary", "arbitrary"),
        name="attention_prompt",
    )(q, k, v)


def _qlat_kernel(q_ref, w_ref, o_ref):
    o_ref[...] = _dot(q_ref[...], w_ref[...]).astype(BF16)


def _qlat_call(q2, wukt):
    n = q2.shape[0]
    r = wukt.shape[2]
    return pl.pallas_call(
        _qlat_kernel,
        grid=(MLA_HEADS,),
        in_specs=[
            pl.BlockSpec((n, HEAD_SLOT), lambda h: (0, h)),
            pl.BlockSpec((None, HEAD_SLOT, r), lambda h: (h, 0, 0)),
        ],
        out_specs=pl.BlockSpec((n, r), lambda h: (0, h)),
        out_shape=jax.ShapeDtypeStruct((n, MLA_HEADS * r), BF16),
        compiler_params=_cparams("arbitrary"),
        name="absorb_q",
    )(q2, wukt)


PV_SPLIT = 4


def _attn_paged_kernel(pt_ref, qlat_ref, q_ref, cnew_ref, pnew_ref, snew_ref, ckv_hbm, kpe_hbm, ks_hbm, o_ref,
                       ckv_buf, kpe_buf, ks_buf, sem, *, layer, cp, n_chunk):
    n_slots = ckv_buf.shape[0]
    b = pl.program_id(0)
    nb = pl.num_programs(0)

    def page_copies(bi, ci, slot, g):
        page = pt_ref[bi, ci * cp + g]
        lanes = pl.ds(pl.multiple_of(g * PAGE_SIZE, PAGE_SIZE), PAGE_SIZE)
        return (pltpu.make_async_copy(ckv_hbm.at[layer, page], ckv_buf.at[slot, g], sem.at[slot, 0]),
                pltpu.make_async_copy(kpe_hbm.at[layer, page], kpe_buf.at[slot, :, lanes], sem.at[slot, 1]),
                pltpu.make_async_copy(ks_hbm.at[layer, page], ks_buf.at[slot, :, lanes], sem.at[slot, 2]))

    def start(bi, ci, slot):
        def one(g, carry):
            for c in page_copies(bi, ci, slot, g):
                c.start()
            return carry
        lax.fori_loop(0, cp, one, 0, unroll=4)

    def start_inline(bi, ci, slot):
        for g in range(cp):
            for c in page_copies(bi, ci, slot, g):
                c.start()

    def wait(slot):
        for j, buf in enumerate((ckv_buf, kpe_buf, ks_buf)):
            pltpu.make_async_copy(buf.at[slot], buf.at[slot], sem.at[slot, j]).wait()

    @pl.when(b == 0)
    def _():
        for ci in range(n_chunk):
            start(0, ci, ci)

    qlat = qlat_ref[...]
    qpe = q_ref[:, NOPE_DIM:QK_DIM]
    h, r = qlat.shape
    m = jnp.full((h, 1), NEG_INF, F32)
    l = jnp.zeros((h, 1), F32)
    accs = [jnp.zeros((h, r), F32) for _ in range(PV_SPLIT)]
    kk = cp * PAGE_SIZE // PV_SPLIT

    def chunk(ci, carry):
        m, l, accs = carry
        slot = lax.rem(b * n_chunk + ci, n_slots)
        wait(slot)
        ckv_bf = ckv_buf[slot].reshape(cp * PAGE_SIZE, r).astype(BF16)
        s_lat = jnp.concatenate([_dot_nt(qlat, ckv_bf[j * kk:(j + 1) * kk]) for j in range(PV_SPLIT)], axis=1)
        s = (s_lat + _dot(qpe, kpe_buf[slot].astype(BF16))) * ks_buf[slot]
        m_new = jnp.maximum(m, jnp.max(s, axis=-1, keepdims=True))
        alpha = jnp.exp2(m - m_new)
        p = jnp.exp2(s - m_new)
        l = alpha * l + jnp.sum(p, axis=-1, keepdims=True)
        p_bf = p.astype(BF16)
        accs = [alpha * a + _dot(p_bf[:, j * kk:(j + 1) * kk], ckv_bf[j * kk:(j + 1) * kk]) for j, a in enumerate(accs)]

        start_inline(lax.rem(b + 1, nb), ci, lax.rem((b + 1) * n_chunk + ci, n_slots))
        return m_new, l, accs

    m, l, accs = lax.fori_loop(0, n_chunk, chunk, (m, l, accs))

    @pl.when(b == nb - 1)
    def _():
        for ci in range(n_chunk):
            wait(lax.rem(nb * n_chunk + ci, n_slots))

    cn = cnew_ref[...]
    s_new = (jnp.sum(qlat.astype(F32) * cn, axis=-1, keepdims=True)
             + jnp.sum(qpe.astype(F32) * pnew_ref[...], axis=-1, keepdims=True)) * snew_ref[...]
    m_new = jnp.maximum(m, s_new)
    alpha = jnp.exp2(m - m_new)
    p_new = jnp.exp2(s_new - m_new)
    l = alpha * l + p_new
    acc = alpha * functools.reduce(lambda x, y: x + y, accs) + p_new * cn
    o_ref[...] = (acc / l).astype(BF16)


def _attn_paged_call(page_table, qlat3, q3, cnew, pnew, snew_t, ckv_pool, kpe_pool_t, ks_pool_t, layer, cp):
    nb, n_pages = page_table.shape
    assert n_pages % cp == 0
    n_chunk = n_pages // cp
    n_slots = n_chunk + 1
    r = ckv_pool.shape[-1]
    grid_spec = pltpu.PrefetchScalarGridSpec(
        num_scalar_prefetch=1,
        grid=(nb,),
        in_specs=[
            pl.BlockSpec((None, MLA_HEADS, r), lambda b, pt: (b, 0, 0)),
            pl.BlockSpec((None, MLA_HEADS, HEAD_SLOT), lambda b, pt: (b, 0, 0)),
            pl.BlockSpec((None, 1, r), lambda b, pt: (b, 0, 0)),
            pl.BlockSpec((None, 1, ROPE_DIM), lambda b, pt: (b, 0, 0)),
            pl.BlockSpec((None, MLA_HEADS, 1), lambda b, pt: (b, 0, 0)),
            pl.BlockSpec(memory_space=pl.ANY),
            pl.BlockSpec(memory_space=pl.ANY),
            pl.BlockSpec(memory_space=pl.ANY),
        ],
        out_specs=pl.BlockSpec((None, MLA_HEADS, r), lambda b, pt: (b, 0, 0)),
        scratch_shapes=[
            pltpu.VMEM((n_slots, cp, PAGE_SIZE, r), F32),
            pltpu.VMEM((n_slots, ROPE_DIM, cp * PAGE_SIZE), F32),
            pltpu.VMEM((n_slots, MLA_HEADS, cp * PAGE_SIZE), F32),
            pltpu.SemaphoreType.DMA((n_slots, 3)),
        ],
    )
    return pl.pallas_call(
        functools.partial(_attn_paged_kernel, layer=layer, cp=cp, n_chunk=n_chunk),
        grid_spec=grid_spec,
        out_shape=jax.ShapeDtypeStruct((nb, MLA_HEADS, r), BF16),
        compiler_params=_cparams("arbitrary"),
        name="attention_paged",
    )(page_table, qlat3, q3, cnew, pnew, snew_t, ckv_pool, kpe_pool_t, ks_pool_t)


def _uv_pair_kernel(o_ref, w_ref, y_ref):
    r = w_ref.shape[1]
    parts = [_dot(o_ref[:, j * r:(j + 1) * r], w_ref[j]) for j in range(w_ref.shape[0])]
    y_ref[...] = jnp.concatenate(parts, axis=-1).astype(BF16)


def _uv_call(olat2, wuv_h):
    n = olat2.shape[0]
    r = wuv_h.shape[1]
    hpb = LANES // V_DIM
    return pl.pallas_call(
        _uv_pair_kernel,
        grid=(MLA_HEADS // hpb,),
        in_specs=[
            pl.BlockSpec((n, hpb * r), lambda h: (0, h)),
            pl.BlockSpec((hpb, r, V_DIM), lambda h: (h, 0, 0)),
        ],
        out_specs=pl.BlockSpec((n, hpb * V_DIM), lambda h: (0, h)),
        out_shape=jax.ShapeDtypeStruct((n, MLA_HEADS * V_DIM), BF16),
        compiler_params=_cparams("arbitrary"),
        name="value_up",
    )(olat2, wuv_h)


def _outproj_kernel(x_ref, sh_ref, sc_ref, gt_ref, g_ref, wg_ref, ret_ref, att_ref, wpa_ref, wpb_ref, wo_ref, o_ref):
    x = x_ref[...]
    d = x.shape[-1]
    h = _modulated_norm(x, g_ref[...], sh_ref[...], sc_ref[...]).astype(BF16)
    gates = _dot(h, wg_ref[...])
    nv = ret_ref.shape[-1]
    rg, ga, gb = gates[:, :nv], gates[:, nv:nv + d], gates[:, nv + d:]
    ya = _dot((ret_ref[...].astype(F32) * (rg * jax.nn.sigmoid(rg))).astype(BF16), wpa_ref[...])
    yb = _dot(att_ref[...], wpb_ref[...])
    merged = jax.nn.sigmoid(ga) * ya + jax.nn.sigmoid(gb) * yb
    o_ref[...] = x + gt_ref[...] * _dot(merged.astype(BF16), wo_ref[...])


def _outproj_call(x3, mod3, norm_g, w_gates, ret_o, att_o, wpa, wpb, wo, tl):
    bx, lx, d = x3.shape
    lm = mod3.shape[1]
    lm_blk = 1 if lm == 1 else tl

    def tok(n):
        return pl.BlockSpec((None, tl, n), lambda b, i: (b, i, 0))

    return pl.pallas_call(
        _outproj_kernel,
        grid=(bx, lx // tl),
        in_specs=[
            tok(d), _mod_spec(lm_blk, d, 3), _mod_spec(lm_blk, d, 4), _mod_spec(lm_blk, d, 5), _const_spec((1, d)),
            _const_spec(w_gates.shape), tok(ret_o.shape[-1]), tok(att_o.shape[-1]),
            _const_spec(wpa.shape), _const_spec(wpb.shape), _const_spec(wo.shape),
        ],
        out_specs=tok(d),
        out_shape=jax.ShapeDtypeStruct(x3.shape, F32),
        compiler_params=_cparams("arbitrary", "arbitrary"),
        name="mixer_outproj",
    )(x3, mod3, mod3, mod3, norm_g.reshape(1, d), w_gates, ret_o, att_o, wpa, wpb, wo)


def _rope_tables(pos):
    half_r = RET_DK // 2
    fr = ROPE_THETA ** (-jnp.arange(half_r, dtype=F32) / half_r)
    ang = pos[:, None] * fr[None, :]
    cos, sin = jnp.cos(ang), jnp.sin(ang)
    cosr = jnp.concatenate([cos, cos], axis=-1)
    sinr = jnp.concatenate([-sin, sin], axis=-1)
    half_m = ROPE_DIM // 2
    fm = ROPE_THETA ** (-jnp.arange(half_m, dtype=F32) / half_m)
    angm = pos[:, None] * fm[None, :]
    cm, sm = jnp.cos(angm), jnp.sin(angm)
    n = pos.shape[0]
    z16, z32 = jnp.zeros((n, half_m), F32), jnp.zeros((n, LANES - QK_DIM), F32)
    rc = jnp.concatenate([jnp.ones((n, NOPE_DIM), F32), cm, cm, z32], axis=-1)
    rsm = jnp.concatenate([jnp.zeros((n, NOPE_DIM), F32), -sm, z16, z32], axis=-1)
    rsp = jnp.concatenate([jnp.zeros((n, NOPE_DIM), F32), z16, sm, z32], axis=-1)
    return cosr, sinr, rc, rsm, rsp


def _retention_tables(chunk):
    log_gamma = jnp.log1p(-jnp.exp2(-5.0 - jnp.arange(RET_HEADS, dtype=F32)))
    idx = jnp.arange(chunk, dtype=F32)
    rel = idx[:, None] - idx[None, :]
    decay = jnp.where(rel[None] >= 0, jnp.exp(rel[None] * log_gamma[:, None, None]), 0.0)
    xi = jnp.exp((idx[None, :] + 1.0) * log_gamma[:, None])[:, :, None]
    zeta = jnp.exp((chunk - 1.0 - idx)[None, :] * log_gamma[:, None])[:, :, None]
    gch = jnp.exp(chunk * log_gamma)[:, None, None]
    return decay, xi, zeta, gch


def _pad_heads(w, width):
    r, hn, n = w.shape
    return jnp.pad(w, ((0, 0), (0, 0), (0, width - n))).reshape(r, hn * width)


def _prep_layer(l, w_in, w_uq, w_uk, w_uv, w_pa, w_pb, w_out, mla_qa_g, mla_kva_g, mla_qn_g, mla_kn_g):
    d = w_in.shape[1]
    rqk, rvw = RET_HEADS * RET_DK, RET_HEADS * RET_DV
    q_rank, kv_rank = w_uq.shape[1], w_uk.shape[1]
    o = np.cumsum([0, rqk, rqk, rvw, rvw, q_rank, kv_rank, ROPE_DIM, d, d])
    wi = w_in[l]
    rq, rk, rv, rg, cq, ckv, kpe, ga, gb = [wi[:, o[i]:o[i + 1]] for i in range(9)]
    kpe_slot = jnp.pad(kpe, ((0, 0), (NOPE_DIM, LANES - QK_DIM)))
    w_a = jnp.concatenate([rv, rq, rk, cq, ckv, kpe_slot], axis=1).astype(BF16)
    offs = tuple(int(v) for v in np.cumsum([0, rvw, rqk, rqk, q_rank, kv_rank, LANES]))
    w_g = jnp.concatenate([rg, ga, gb], axis=1).astype(BF16)
    wuq = _pad_heads(w_uq[l], HEAD_SLOT).astype(BF16)
    wuk = _pad_heads(w_uk[l], HEAD_SLOT).astype(BF16)
    wuv = w_uv[l].reshape(kv_rank, MLA_HEADS * V_DIM).astype(BF16)
    wukt = jnp.pad(jnp.transpose(w_uk[l], (1, 2, 0)), ((0, 0), (0, HEAD_SLOT - NOPE_DIM), (0, 0))).astype(BF16)
    wuv_h = jnp.transpose(w_uv[l], (1, 0, 2)).astype(BF16)
    gqn, gkn = mla_qn_g[l], mla_kn_g[l]
    zpad = jnp.zeros((LANES - QK_DIM,), F32)
    aq = (jnp.concatenate([gqn[:NOPE_DIM] * gkn[:NOPE_DIM], gqn[NOPE_DIM:], zpad]) * (MLA_SCALE * LOG2E)).reshape(1, LANES)
    ak = jnp.concatenate([jnp.zeros((NOPE_DIM,), F32), gkn[NOPE_DIM:], zpad]).reshape(1, LANES)
    gains = (mla_qa_g[l].reshape(1, q_rank), mla_kva_g[l].reshape(1, kv_rank), aq, ak)
    return dict(w_a=w_a, offs=offs, w_g=w_g, wuq=wuq, wuk=wuk, wuv=wuv, wukt=wukt, wuv_h=wuv_h, gains=gains,
                wpa=w_pa[l].astype(BF16), wpb=w_pb[l].astype(BF16), wo=w_out[l].astype(BF16))


def _largest_tile(n, cap):
    t = min(n, cap)
    while n % t:
        t //= 2
    return t


def kernel(x_prompt, x_sample, cache_ckv, cache_kpe, cache_kscale, state_ret, page_table, c_prompt, c_sample, ada_w, ada_b, norm_g, ffn_w13, ffn_w2, w_in, ret_gn_g, mla_qa_g, mla_kva_g, w_uq, mla_qn_g, mla_kn_g, w_uk, w_uv, w_pa, w_pb, w_out):
    depth = ada_w.shape[0]
    bp, lp, d = x_prompt.shape
    ns, ls, _ = x_sample.shape
    assert ls == 1, "sample group is one new token per sequence"
    assert lp % RET_CHUNK == 0
    n_pages = page_table.shape[1]
    past_len = n_pages * PAGE_SIZE

    mod = _mod_call(jnp.concatenate([c_prompt, c_sample], axis=0), ada_w, ada_b)
    tabs_p = _rope_tables(jnp.arange(lp, dtype=F32))
    tabs_s = _rope_tables(jnp.full((ns,), float(past_len), F32))
    rt_p = _retention_tables(RET_CHUNK)
    rt_s = _retention_tables(1)

    tl_p = _largest_tile(lp, 512)
    tl_in = _largest_tile(lp, 256)
    tq = _largest_tile(lp, 512)
    bb = _largest_tile(ns, 8)
    nbt = _largest_tile(bp, 1)
    rcs = _largest_tile(lp // RET_CHUNK, 4)
    cp = _largest_tile(n_pages // 2, 64)
    kpe_pool_t = jnp.swapaxes(cache_kpe, 2, 3)
    ks_pool_t = jnp.swapaxes(cache_kscale, 2, 3)

    w13_all = ffn_w13.astype(BF16)
    w2_all = ffn_w2.astype(BF16)

    yp = x_prompt
    ys = x_sample.reshape(1, ns, d)
    st_p, st_s = [], []
    ret_s_new = []
    for l in range(depth):
        w = _prep_layer(l, w_in, w_uq, w_uk, w_uv, w_pa, w_pb, w_out, mla_qa_g, mla_kva_g, mla_qn_g, mla_kn_g)
        mod_p = mod[l, :bp].reshape(bp, 1, N_MOD * d)
        mod_s = mod[l, bp:].reshape(1, ns, N_MOD * d)
        ng = norm_g[l]

        yp = _ffn_call(yp, mod_p, 0, ng[0], w13_all, w2_all, l, 0, tl_p)
        rq, rk, rv, q, ckv, kpe, ks, kf, vf = _inproj_call(
            yp, mod_p, ng[1], w["w_a"], w["offs"], tabs_p, w["gains"], w["wuq"], w["wuk"], w["wuv"], tl_in, True)
        ret_o, ret_s = _ret_prompt_call(rq, rk, rv, rt_p, ret_gn_g[l], nbt, rcs)
        att_o = _attn_prompt_call(q, kf, vf, tq)
        yp = _outproj_call(yp, mod_p, ng[1], w["w_g"], ret_o, att_o, w["wpa"], w["wpb"], w["wo"], tl_p)
        yp = _ffn_call(yp, mod_p, 6, ng[2], w13_all, w2_all, l, 1, tl_p)
        st_p.append((ckv, kpe, ks, ret_s))

        ys = _ffn_call(ys, mod_s, 0, ng[0], w13_all, w2_all, l, 0, ns)
        rq, rk, rv, q, ckv, kpe, ks = _inproj_call(
            ys, mod_s, ng[1], w["w_a"], w["offs"], tabs_s, w["gains"], w["wuq"], w["wuk"], w["wuv"], ns, False)
        last = l == depth - 1
        ret_o, ret_s = _ret_sample_call(rq[0], rk[0], rv[0], state_ret, ret_s_new if last else [], l, rt_s,
                                        ret_gn_g[l], bb)
        ret_s_new.append(ret_s)
        qlat = _qlat_call(q[0], w["wukt"])

        olat = _attn_paged_call(page_table, qlat.reshape(ns, MLA_HEADS, -1), q[0].reshape(ns, MLA_HEADS, HEAD_SLOT),
                                ckv.reshape(ns, 1, -1), kpe.reshape(ns, 1, -1), ks.reshape(ns, MLA_HEADS, 1),
                                cache_ckv, kpe_pool_t, ks_pool_t, l, cp)
        att_o = _uv_call(olat.reshape(ns, -1), w["wuv_h"])
        ys = _outproj_call(ys, mod_s, ng[1], w["w_g"], ret_o[None], att_o[None], w["wpa"], w["wpb"], w["wo"], ns)
        ys = _ffn_call(ys, mod_s, 6, ng[2], w13_all, w2_all, l, 1, ns)
        st_s.append((ckv.reshape(ns, 1, -1), kpe.reshape(ns, 1, -1), ks.reshape(ns, 1, -1)))

    def stack(states, i):
        return jnp.stack([st[i] for st in states], axis=0)

    return (yp, ys.reshape(ns, 1, d),
            stack(st_p, 0), stack(st_p, 1), stack(st_p, 2), stack(st_p, 3),
            stack(st_s, 0), stack(st_s, 1), stack(st_s, 2),
            ret_s_new[-1] if depth > 1 else ret_s_new[-1][None])
```

```python
import functools

import jax
import jax.numpy as jnp
import numpy as np
from jax import lax
from jax.experimental import pallas as pl
from jax.experimental.pallas import tpu as pltpu

F32 = jnp.float32
BF16 = jnp.bfloat16

RET_HEADS = 4
RET_DK = 128
RET_DV = 256
RET_CHUNK = 128
MLA_HEADS = 16
NOPE_DIM = 64
ROPE_DIM = 32
V_DIM = 64
QK_DIM = NOPE_DIM + ROPE_DIM
MLA_SCALE = QK_DIM ** -0.5
LOG2E = float(np.log2(np.e))
ROPE_THETA = 10000.0
N_MOD = 9
EPS = 1e-6
PAGE_SIZE = 128

LANES = 128
MXU_DEPTH = 256
HEAD_SLOT = LANES
VMEM_LIMIT = 56 * 1024 * 1024
NEG_INF = float("-inf")


def _cparams(*sem):
    return pltpu.CompilerParams(dimension_semantics=sem, vmem_limit_bytes=VMEM_LIMIT)


def _dot(a, b):
    return jnp.dot(a, b, preferred_element_type=F32)


def _dot_nt(a, b):
    return lax.dot_general(a, b, (((1,), (1,)), ((), ())), preferred_element_type=F32)


def _dot_tn(a, b):
    return lax.dot_general(a, b, (((0,), (0,)), ((), ())), preferred_element_type=F32)


def _const_spec(shape):
    n = len(shape)
    return pl.BlockSpec(shape, lambda *_: (0,) * n)


def _modulated_norm(x, g, shift, scale):
    ms = jnp.mean(x * x, axis=-1, keepdims=True)
    y = x * lax.rsqrt(ms + EPS) * g
    return y * (1.0 + scale) + shift


def _mod_spec(lm_blk, d, k):
    if lm_blk == 1:
        return pl.BlockSpec((None, 1, d), lambda b, i: (b, 0, k))
    return pl.BlockSpec((None, lm_blk, d), lambda b, i: (b, i, k))


def _mod_kernel(c_ref, w_ref, b_ref, o_ref):
    c = c_ref[...]
    s = (c * jax.nn.sigmoid(c)).astype(BF16)
    o_ref[...] = _dot(s, w_ref[...].astype(BF16)) + b_ref[...]


def _mod_call(c_all, ada_w, ada_b):
    depth, d, nd = ada_w.shape
    n = c_all.shape[0]
    return pl.pallas_call(
        _mod_kernel,
        grid=(depth, nd // d),
        in_specs=[
            pl.BlockSpec((n, d), lambda l, j: (0, 0)),
            pl.BlockSpec((None, d, d), lambda l, j: (l, 0, j)),
            pl.BlockSpec((None, 1, d), lambda l, j: (l, 0, j)),
        ],
        out_specs=pl.BlockSpec((None, n, d), lambda l, j: (l, 0, j)),
        out_shape=jax.ShapeDtypeStruct((depth, n, nd), F32),
        compiler_params=_cparams("arbitrary", "arbitrary"),
        name="adaln_mod",
    )(c_all, ada_w, ada_b.reshape(depth, 1, nd))


def _ffn_kernel(x_ref, sh_ref, sc_ref, gt_ref, g_ref, w13_ref, w2_ref, o_ref, *, d_ff):
    x = x_ref[...]
    h = _modulated_norm(x, g_ref[...], sh_ref[...], sc_ref[...]).astype(BF16)
    ab = _dot(h, w13_ref[...])
    a, b = ab[:, :d_ff], ab[:, d_ff:]
    u = (a * jax.nn.sigmoid(a) * b).astype(BF16)
    y = _dot(u, w2_ref[...])
    o_ref[...] = x + 0.5 * gt_ref[...] * y


def _ffn_call(x3, mod3, k0, norm_g, w13_all, w2_all, layer, j, tl):
    bx, lx, d = x3.shape
    lm = mod3.shape[1]
    lm_blk = 1 if lm == 1 else tl
    d_ff = w2_all.shape[2]
    return pl.pallas_call(
        functools.partial(_ffn_kernel, d_ff=d_ff),
        grid=(bx, lx // tl),
        in_specs=[
            pl.BlockSpec((None, tl, d), lambda b, i: (b, i, 0)),
            _mod_spec(lm_blk, d, k0), _mod_spec(lm_blk, d, k0 + 1), _mod_spec(lm_blk, d, k0 + 2),
            _const_spec((1, d)),
            pl.BlockSpec((None, None) + w13_all.shape[2:], lambda b, i: (layer, j, 0, 0), pipeline_mode=pl.Buffered(1)),
            pl.BlockSpec((None, None) + w2_all.shape[2:], lambda b, i: (layer, j, 0, 0), pipeline_mode=pl.Buffered(1)),
        ],
        out_specs=pl.BlockSpec((None, tl, d), lambda b, i: (b, i, 0)),
        out_shape=jax.ShapeDtypeStruct(x3.shape, F32),
        compiler_params=_cparams("arbitrary", "arbitrary"),
        name="ffn",
    )(x3, mod3, mod3, mod3, norm_g.reshape(1, d), w13_all, w2_all)


def _rope_slot(t, c, sm, sp):
    return t * c + pltpu.roll(t, LANES - ROPE_DIM // 2, axis=1) * sm + pltpu.roll(t, ROPE_DIM // 2, axis=1) * sp


def _slot_sumsq(v, ones_ref):
    w = ones_ref.shape[0]
    ones_bd = ones_ref[...]
    return jnp.concatenate([_dot((v[:, j:j + w] * v[:, j:j + w]).astype(BF16), ones_bd)
                            for j in range(0, v.shape[1], w)], axis=1)


def _inproj_kernel(x_ref, sh_ref, sc_ref, g_ref, w_ref, cosr_ref, sinr_ref, rc_ref, rsm_ref, rsp_ref,
                   gqa_ref, gkva_ref, aq_ref, ak_ref, wuq_ref, wuk_ref, wuv_ref, ones_ref,
                   rq_ref, rk_ref, rv_ref, q_ref, ckv_ref, kpe_ref, ks_ref, *kv_refs, offs):
    _inproj_rows(slice(0, x_ref.shape[0]), x_ref, sh_ref, sc_ref, g_ref, w_ref, cosr_ref, sinr_ref,
                 rc_ref, rsm_ref, rsp_ref, gqa_ref, gkva_ref, aq_ref, ak_ref, wuq_ref, wuk_ref, wuv_ref, ones_ref,
                 rq_ref, rk_ref, rv_ref, q_ref, ckv_ref, kpe_ref, ks_ref, kv_refs, offs)


def _inproj_rows(rows, x_ref, sh_ref, sc_ref, g_ref, w_ref, cosr_ref, sinr_ref, rc_ref, rsm_ref, rsp_ref,
                 gqa_ref, gkva_ref, aq_ref, ak_ref, wuq_ref, wuk_ref, wuv_ref, ones_ref,
                 rq_ref, rk_ref, rv_ref, q_ref, ckv_ref, kpe_ref, ks_ref, kv_refs, offs):
    o_rv, o_rq, o_rk, o_cq, o_ckv, o_kpe, n_in = offs

    def mod_rows(ref):
        return ref[...] if ref.shape[0] == 1 else ref[rows, :]

    x = x_ref[rows, :]
    h = _modulated_norm(x, g_ref[...], mod_rows(sh_ref), mod_rows(sc_ref)).astype(BF16)
    p = _dot(h, w_ref[...])

    rv_ref[rows, :] = p[:, o_rv:o_rq].astype(BF16)
    cosr, sinr = cosr_ref[rows, :], sinr_ref[rows, :]
    for hh in range(RET_HEADS):
        lo, hi = hh * RET_DK, (hh + 1) * RET_DK
        tq = p[:, o_rq + lo:o_rq + hi]
        rq_ref[rows, lo:hi] = (tq * cosr + pltpu.roll(tq, RET_DK // 2, axis=1) * sinr).astype(BF16)
        tk = p[:, o_rk + lo:o_rk + hi]
        rk_ref[rows, lo:hi] = ((tk * cosr + pltpu.roll(tk, RET_DK // 2, axis=1) * sinr) * (RET_DK ** -0.5)).astype(BF16)

    rc, rsm, rsp = rc_ref[rows, :], rsm_ref[rows, :], rsp_ref[rows, :]

    cq = p[:, o_cq:o_ckv]
    cq = (cq * lax.rsqrt(jnp.mean(cq * cq, axis=-1, keepdims=True) + EPS) * gqa_ref[...]).astype(BF16)
    qf = _dot(cq, wuq_ref[...])
    qn = qf * lax.rsqrt(_slot_sumsq(qf, ones_ref) * (1.0 / QK_DIM) + EPS) * aq_ref[...]
    for hh in range(MLA_HEADS):
        lo, hi = hh * HEAD_SLOT, (hh + 1) * HEAD_SLOT
        q_ref[rows, lo:hi] = _rope_slot(qn[:, lo:hi], rc, rsm, rsp).astype(BF16)

    craw = p[:, o_ckv:o_kpe]
    ckv = craw * lax.rsqrt(jnp.mean(craw * craw, axis=-1, keepdims=True) + EPS) * gkva_ref[...]
    ckv_ref[rows, :] = ckv
    ckv_bf = ckv.astype(BF16)
    kn = _dot(ckv_bf, wuk_ref[...])
    kblk = p[:, o_kpe:n_in]
    kpe_ss = _dot((kblk * kblk).astype(BF16), ones_ref[:HEAD_SLOT, :HEAD_SLOT])
    kpe_rot = _rope_slot(kblk * ak_ref[...], rc, rsm, rsp)
    kpe_ref[rows, :] = kpe_rot[:, NOPE_DIM:QK_DIM]
    ms = (_slot_sumsq(kn, ones_ref) + _lane_tile(kpe_ss, MLA_HEADS)) * (1.0 / QK_DIM)
    ks_rep = lax.rsqrt(ms + EPS)
    lane16 = lax.broadcasted_iota(jnp.int32, (x.shape[0], MLA_HEADS), 1)
    ks_all = jnp.zeros((x.shape[0], MLA_HEADS), F32)
    for hh in range(MLA_HEADS):
        ks_all = jnp.where(lane16 == hh, ks_rep[:, hh * HEAD_SLOT:hh * HEAD_SLOT + MLA_HEADS], ks_all)
    ks_ref[rows, :] = ks_all
    if kv_refs:
        kv_refs[0][rows, :] = ((kn + _lane_tile(kpe_rot, MLA_HEADS)) * ks_rep).astype(BF16)
        kv_refs[1][rows, :] = _dot(ckv_bf, wuv_ref[...]).astype(BF16)


def _inproj_call(x3, mod3, norm_g, w_in_a, offs, tabs, gains, wuq, wuk, wuv, tl, emit_kv):
    bx, lx, d = x3.shape
    lm = mod3.shape[1]
    lm_blk = 1 if lm == 1 else tl
    cosr, sinr, rc, rsm, rsp = tabs
    gqa, gkva, aq, ak = gains
    hs = MLA_HEADS * HEAD_SLOT
    aq = jnp.tile(aq, (1, MLA_HEADS))
    slot_id = jnp.arange(MXU_DEPTH) // HEAD_SLOT
    ones_bd = (slot_id[:, None] == slot_id[None, :]).astype(BF16)

    def tok(n):
        return pl.BlockSpec((None, tl, n), lambda b, i: (b, i, 0))

    def tab():
        return pl.BlockSpec((tl, LANES), lambda b, i: (i, 0))

    def out(n, dt):
        return jax.ShapeDtypeStruct((bx, lx, n), dt)

    out_specs = [tok(RET_HEADS * RET_DK), tok(RET_HEADS * RET_DK), tok(RET_HEADS * RET_DV), tok(hs),
                 tok(wuk.shape[0]), tok(ROPE_DIM), tok(MLA_HEADS)]
    out_shape = [out(RET_HEADS * RET_DK, BF16), out(RET_HEADS * RET_DK, BF16), out(RET_HEADS * RET_DV, BF16),
                 out(hs, BF16), out(wuk.shape[0], F32), out(ROPE_DIM, F32), out(MLA_HEADS, F32)]
    if emit_kv:
        out_specs += [tok(hs), tok(MLA_HEADS * V_DIM)]
        out_shape += [out(hs, BF16), out(MLA_HEADS * V_DIM, BF16)]
    return pl.pallas_call(
        functools.partial(_inproj_kernel, offs=offs),
        grid=(bx, lx // tl),
        in_specs=[
            tok(d), _mod_spec(lm_blk, d, 3), _mod_spec(lm_blk, d, 4), _const_spec((1, d)),
            _const_spec(w_in_a.shape), tab(), tab(), tab(), tab(), tab(),
            _const_spec(gqa.shape), _const_spec(gkva.shape), _const_spec(aq.shape), _const_spec(ak.shape),
            _const_spec(wuq.shape), _const_spec(wuk.shape), _const_spec(wuv.shape), _const_spec(ones_bd.shape),
        ],
        out_specs=out_specs,
        out_shape=out_shape,
        compiler_params=_cparams("arbitrary", "arbitrary"),
        name="mixer_inproj",
    )(x3, mod3, mod3, norm_g.reshape(1, d), w_in_a, cosr, sinr, rc, rsm, rsp, gqa, gkva, aq, ak, wuq, wuk, wuv,
      ones_bd)


def _group_norm(o, g):
    mu = jnp.mean(o, axis=-1, keepdims=True)
    oc = o - mu
    var = jnp.mean(oc * oc, axis=-1, keepdims=True)
    return oc * lax.rsqrt(var + EPS) * g


def _ret_prompt_kernel(q_ref, k_ref, v_ref, dec_ref, xi_ref, zeta_ref, gch_ref, gn_ref, o_ref, sout_ref, s_scr):
    c = pl.program_id(1)

    @pl.when(c == 0)
    def _():
        s_scr[...] = jnp.zeros_like(s_scr)

    ch = dec_ref.shape[1]
    for bi in range(q_ref.shape[0]):
        for hh in range(RET_HEADS):
            s = s_scr[bi, hh]
            for ci in range(q_ref.shape[1] // ch):
                rows = slice(ci * ch, (ci + 1) * ch)
                q = q_ref[bi, rows, hh * RET_DK:(hh + 1) * RET_DK]
                k = k_ref[bi, rows, hh * RET_DK:(hh + 1) * RET_DK]
                v = v_ref[bi, rows, hh * RET_DV:(hh + 1) * RET_DV]
                scores = _dot_nt(q, k) * dec_ref[hh]
                inner = _dot(scores.astype(BF16), v)
                cross = _dot(q, s.astype(BF16)) * xi_ref[hh]
                kz = (k.astype(F32) * zeta_ref[hh]).astype(BF16)
                s = gch_ref[hh] * s + _dot_tn(kz, v)
                o_ref[bi, rows, hh * RET_DV:(hh + 1) * RET_DV] = _group_norm(inner + cross, gn_ref[hh]).astype(BF16)
            s_scr[bi, hh] = s

    @pl.when(c == pl.num_programs(1) - 1)
    def _():
        sout_ref[...] = s_scr[...]


def _ret_prompt_call(rq, rk, rv, ret_tabs, gn_g, nbt, chunks_per_step):
    bx, lx, _ = rq.shape
    ch = RET_CHUNK * chunks_per_step
    dec, xi, zeta, gch = ret_tabs
    return pl.pallas_call(
        _ret_prompt_kernel,
        grid=(bx // nbt, lx // ch),
        in_specs=[
            pl.BlockSpec((nbt, ch, RET_HEADS * RET_DK), lambda b, c: (b, c, 0)),
            pl.BlockSpec((nbt, ch, RET_HEADS * RET_DK), lambda b, c: (b, c, 0)),
            pl.BlockSpec((nbt, ch, RET_HEADS * RET_DV), lambda b, c: (b, c, 0)),
            _const_spec(dec.shape), _const_spec(xi.shape), _const_spec(zeta.shape), _const_spec(gch.shape),
            _const_spec((RET_HEADS, 1, RET_DV)),
        ],
        out_specs=[
            pl.BlockSpec((nbt, ch, RET_HEADS * RET_DV), lambda b, c: (b, c, 0)),
            pl.BlockSpec((nbt, RET_HEADS, RET_DK, RET_DV), lambda b, c: (b, 0, 0, 0)),
        ],
        out_shape=[
            jax.ShapeDtypeStruct((bx, lx, RET_HEADS * RET_DV), BF16),
            jax.ShapeDtypeStruct((bx, RET_HEADS, RET_DK, RET_DV), F32),
        ],
        scratch_shapes=[pltpu.VMEM((nbt, RET_HEADS, RET_DK, RET_DV), F32)],
        compiler_params=_cparams("arbitrary", "arbitrary"),
        name="retention_prompt",
    )(rq, rk, rv, dec, xi, zeta, gch, gn_g.reshape(RET_HEADS, 1, RET_DV))


def _ret_sample_kernel(qt_ref, kt_ref, v_ref, s_ref, dec_ref, xi_ref, zeta_ref, gch_ref, gn_ref, *rest, bb):
    prev_refs, (o_ref, sout_ref, pre_scr) = rest[:-3], rest[-3:]
    for li, prev_ref in enumerate(prev_refs):
        sout_ref[li] = prev_ref[...]
    own = sout_ref.at[len(prev_refs)] if prev_refs else sout_ref
    for j in range(bb):
        for hh in range(RET_HEADS):
            qcol = qt_ref[hh * RET_DK:(hh + 1) * RET_DK, j:j + 1].astype(F32)
            kcol = kt_ref[hh * RET_DK:(hh + 1) * RET_DK, j:j + 1].astype(F32)
            vrow = v_ref[j:j + 1, hh * RET_DV:(hh + 1) * RET_DV].astype(F32)
            s0 = s_ref[j, hh]
            score = jnp.sum(qcol * kcol, axis=0, keepdims=True) * dec_ref[hh]
            inner = score * vrow
            cross = jnp.sum(qcol * s0, axis=0, keepdims=True) * xi_ref[hh]
            own[j, hh] = gch_ref[hh] * s0 + (kcol * zeta_ref[hh]) * vrow
            pre_scr[j:j + 1, hh * RET_DV:(hh + 1) * RET_DV] = inner + cross
    for hh in range(RET_HEADS):
        cols = slice(hh * RET_DV, (hh + 1) * RET_DV)
        o_ref[:, cols] = _group_norm(pre_scr[:, cols], gn_ref[hh]).astype(BF16)


def _ret_sample_call(rq, rk, rv, state_all, prev_new, layer, ret_tabs, gn_g, bb):
    n = rq.shape[0]
    dec, xi, zeta, gch = ret_tabs
    dqk = RET_HEADS * RET_DK
    qt = rq.reshape(n // bb, bb, dqk).transpose(0, 2, 1)
    kt = rk.reshape(n // bb, bb, dqk).transpose(0, 2, 1)
    st_blk = (bb, RET_HEADS, RET_DK, RET_DV)
    if prev_new:
        n_out = len(prev_new) + 1
        st_spec = pl.BlockSpec((n_out,) + st_blk, lambda i: (0, i, 0, 0, 0))
        st_shape = jax.ShapeDtypeStruct((n_out,) + state_all.shape[1:], F32)
    else:
        st_spec = pl.BlockSpec(st_blk, lambda i: (i, 0, 0, 0))
        st_shape = jax.ShapeDtypeStruct(state_all.shape[1:], F32)
    return pl.pallas_call(
        functools.partial(_ret_sample_kernel, bb=bb),
        grid=(n // bb,),
        in_specs=[
            pl.BlockSpec((None, dqk, bb), lambda i: (i, 0, 0)),
            pl.BlockSpec((None, dqk, bb), lambda i: (i, 0, 0)),
            pl.BlockSpec((bb, RET_HEADS * RET_DV), lambda i: (i, 0)),
            pl.BlockSpec((None,) + st_blk, lambda i: (layer, i, 0, 0, 0)),
            _const_spec(dec.shape), _const_spec(xi.shape), _const_spec(zeta.shape), _const_spec(gch.shape),
            _const_spec((RET_HEADS, 1, RET_DV)),
        ] + [pl.BlockSpec(st_blk, lambda i: (i, 0, 0, 0)) for _ in prev_new],
        out_specs=[pl.BlockSpec((bb, RET_HEADS * RET_DV), lambda i: (i, 0)), st_spec],
        out_shape=[jax.ShapeDtypeStruct((n, RET_HEADS * RET_DV), BF16), st_shape],
        scratch_shapes=[pltpu.VMEM((bb, RET_HEADS * RET_DV), F32)],
        compiler_params=_cparams("arbitrary"),
        name="retention_sample",
    )(qt, kt, rv, state_all, dec, xi, zeta, gch, gn_g.reshape(RET_HEADS, 1, RET_DV), *prev_new)


HEADS_PER_GROUP = 4


def _lane_tile(x, n):
    return x if n == 1 else jnp.concatenate([x] * n, axis=1)


def _attn_prompt_kernel(q_ref, k_ref, v_ref, o_ref, *, tq, nq):
    i = pl.program_id(2)
    half = tq // 2 if (tq // 2) % LANES == 0 else tq
    vw = o_ref.shape[-1]
    vrep = vw // LANES
    row = lax.broadcasted_iota(jnp.int32, (half, half), 0)
    col = lax.broadcasted_iota(jnp.int32, (half, half), 1)
    tri = jnp.where(row >= col, 0.0, NEG_INF).astype(F32)
    lane = lax.broadcasted_iota(jnp.int32, (half, vw), 1)

    def rows_block(r0, n):
        rows = slice(r0, r0 + half)
        vblk = v_ref[0:n, :]
        o = jnp.zeros((half, vw), F32)
        for hh in range(HEADS_PER_GROUP):
            s = _dot_nt(q_ref[rows, hh * HEAD_SLOT:(hh + 1) * HEAD_SLOT], k_ref[0:n, hh * HEAD_SLOT:(hh + 1) * HEAD_SLOT])
            s = jnp.concatenate([s[:, :n - half], s[:, n - half:] + tri], axis=1) if n > half else s + tri
            m = jnp.broadcast_to(jnp.max(s, axis=-1, keepdims=True), (half, LANES))
            p = jnp.exp2(s - _lane_tile(m, n // LANES))
            l = jnp.broadcast_to(jnp.sum(p, axis=-1, keepdims=True), (half, LANES))
            oh = _dot(p.astype(BF16), vblk) * _lane_tile(1.0 / l, vrep)
            o = jnp.where((lane >= hh * V_DIM) & (lane < (hh + 1) * V_DIM), oh, o)
        o_ref[rows, :] = o.astype(BF16)

    for j in range(nq):
        @pl.when(i == j)
        def _():
            for r0 in range(0, tq, half):
                rows_block(r0, j * tq + r0 + half)


def _attn_prompt_call(q, k, v, tq):
    bx, lx, _ = q.shape
    gw = HEADS_PER_GROUP * HEAD_SLOT
    vw = HEADS_PER_GROUP * V_DIM
    ng = MLA_HEADS // HEADS_PER_GROUP
    return pl.pallas_call(
        functools.partial(_attn_prompt_kernel, tq=tq, nq=lx // tq),
        grid=(bx, ng, lx // tq),
        in_specs=[
            pl.BlockSpec((None, tq, gw), lambda b, g, i: (b, i, g)),
            pl.BlockSpec((None, lx, gw), lambda b, g, i: (b, 0, g)),
            pl.BlockSpec((None, lx, vw), lambda b, g, i: (b, 0, g)),
        ],
        out_specs=pl.BlockSpec((None, tq, vw), lambda b, g, i: (b, i, g)),
        out_shape=jax.ShapeDtypeStruct((bx, lx, MLA_HEADS * V_DIM), BF16),
        compiler_params=_cparams("arbitrary", "arbitrary", "arbitrary"),
        name="attention_prompt",
    )(q, k, v)


def _qlat_kernel(q_ref, w_ref, o_ref):
    o_ref[...] = _dot(q_ref[...], w_ref[...]).astype(BF16)


def _qlat_call(q2, wukt):
    n = q2.shape[0]
    r = wukt.shape[2]
    return pl.pallas_call(
        _qlat_kernel,
        grid=(MLA_HEADS,),
        in_specs=[
            pl.BlockSpec((n, HEAD_SLOT), lambda h: (0, h)),
            pl.BlockSpec((None, HEAD_SLOT, r), lambda h: (h, 0, 0)),
        ],
        out_specs=pl.BlockSpec((n, r), lambda h: (0, h)),
        out_shape=jax.ShapeDtypeStruct((n, MLA_HEADS * r), BF16),
        compiler_params=_cparams("arbitrary"),
        name="absorb_q",
    )(q2, wukt)


PV_SPLIT = 4


def _attn_paged_kernel(pt_ref, qlat_ref, q_ref, cnew_ref, pnew_ref, snew_ref, ckv_hbm, kpe_hbm, ks_hbm, o_ref,
                       ckv_buf, kpe_buf, ks_buf, sem, *, layer, cp, n_chunk):
    n_slots = ckv_buf.shape[0]
    b = pl.program_id(0)
    nb = pl.num_programs(0)

    def page_copies(bi, ci, slot, g):
        page = pt_ref[bi, ci * cp + g]
        lanes = pl.ds(pl.multiple_of(g * PAGE_SIZE, PAGE_SIZE), PAGE_SIZE)
        return (pltpu.make_async_copy(ckv_hbm.at[layer, page], ckv_buf.at[slot, g], sem.at[slot, 0]),
                pltpu.make_async_copy(kpe_hbm.at[layer, page], kpe_buf.at[slot, :, lanes], sem.at[slot, 1]),
                pltpu.make_async_copy(ks_hbm.at[layer, page], ks_buf.at[slot, :, lanes], sem.at[slot, 2]))

    def start(bi, ci, slot):
        def one(g, carry):
            for c in page_copies(bi, ci, slot, g):
                c.start()
            return carry
        lax.fori_loop(0, cp, one, 0, unroll=4)

    def start_inline(bi, ci, slot):
        for g in range(cp):
            for c in page_copies(bi, ci, slot, g):
                c.start()

    def wait(slot):
        for j, buf in enumerate((ckv_buf, kpe_buf, ks_buf)):
            pltpu.make_async_copy(buf.at[slot], buf.at[slot], sem.at[slot, j]).wait()

    @pl.when(b == 0)
    def _():
        for ci in range(n_chunk):
            start(0, ci, ci)

    qlat = qlat_ref[...]
    qpe = q_ref[:, NOPE_DIM:QK_DIM]
    h, r = qlat.shape
    m = jnp.full((h, 1), NEG_INF, F32)
    l = jnp.zeros((h, 1), F32)
    accs = [jnp.zeros((h, r), F32) for _ in range(PV_SPLIT)]
    kk = cp * PAGE_SIZE // PV_SPLIT

    def chunk(ci, carry):
        m, l, accs = carry
        slot = lax.rem(b * n_chunk + ci, n_slots)
        wait(slot)
        ckv_bf = ckv_buf[slot].reshape(cp * PAGE_SIZE, r).astype(BF16)
        s_lat = jnp.concatenate([_dot_nt(qlat, ckv_bf[j * kk:(j + 1) * kk]) for j in range(PV_SPLIT)], axis=1)
        s = (s_lat + _dot(qpe, kpe_buf[slot].astype(BF16))) * ks_buf[slot]
        m_new = jnp.maximum(m, jnp.max(s, axis=-1, keepdims=True))
        alpha = jnp.exp2(m - m_new)
        p = jnp.exp2(s - m_new)
        l = alpha * l + jnp.sum(p, axis=-1, keepdims=True)
        p_bf = p.astype(BF16)
        accs = [alpha * a + _dot(p_bf[:, j * kk:(j + 1) * kk], ckv_bf[j * kk:(j + 1) * kk]) for j, a in enumerate(accs)]

        start_inline(lax.rem(b + 1, nb), ci, lax.rem((b + 1) * n_chunk + ci, n_slots))
        return m_new, l, accs

    m, l, accs = lax.fori_loop(0, n_chunk, chunk, (m, l, accs))

    @pl.when(b == nb - 1)
    def _():
        for ci in range(n_chunk):
            wait(lax.rem(nb * n_chunk + ci, n_slots))

    cn = cnew_ref[...]
    s_new = (jnp.sum(qlat.astype(F32) * cn, axis=-1, keepdims=True)
             + jnp.sum(qpe.astype(F32) * pnew_ref[...], axis=-1, keepdims=True)) * snew_ref[...]
    m_new = jnp.maximum(m, s_new)
    alpha = jnp.exp2(m - m_new)
    p_new = jnp.exp2(s_new - m_new)
    l = alpha * l + p_new
    acc = alpha * functools.reduce(lambda x, y: x + y, accs) + p_new * cn
    o_ref[...] = (acc / l).astype(BF16)


def _attn_paged_call(page_table, qlat3, q3, cnew, pnew, snew_t, ckv_pool, kpe_pool_t, ks_pool_t, layer, cp):
    nb, n_pages = page_table.shape
    assert n_pages % cp == 0
    n_chunk = n_pages // cp
    n_slots = n_chunk + 1
    r = ckv_pool.shape[-1]
    grid_spec = pltpu.PrefetchScalarGridSpec(
        num_scalar_prefetch=1,
        grid=(nb,),
        in_specs=[
            pl.BlockSpec((None, MLA_HEADS, r), lambda b, pt: (b, 0, 0)),
            pl.BlockSpec((None, MLA_HEADS, HEAD_SLOT), lambda b, pt: (b, 0, 0)),
            pl.BlockSpec((None, 1, r), lambda b, pt: (b, 0, 0)),
            pl.BlockSpec((None, 1, ROPE_DIM), lambda b, pt: (b, 0, 0)),
            pl.BlockSpec((None, MLA_HEADS, 1), lambda b, pt: (b, 0, 0)),
            pl.BlockSpec(memory_space=pl.ANY),
            pl.BlockSpec(memory_space=pl.ANY),
            pl.BlockSpec(memory_space=pl.ANY),
        ],
        out_specs=pl.BlockSpec((None, MLA_HEADS, r), lambda b, pt: (b, 0, 0)),
        scratch_shapes=[
            pltpu.VMEM((n_slots, cp, PAGE_SIZE, r), F32),
            pltpu.VMEM((n_slots, ROPE_DIM, cp * PAGE_SIZE), F32),
            pltpu.VMEM((n_slots, MLA_HEADS, cp * PAGE_SIZE), F32),
            pltpu.SemaphoreType.DMA((n_slots, 3)),
        ],
    )
    return pl.pallas_call(
        functools.partial(_attn_paged_kernel, layer=layer, cp=cp, n_chunk=n_chunk),
        grid_spec=grid_spec,
        out_shape=jax.ShapeDtypeStruct((nb, MLA_HEADS, r), BF16),
        compiler_params=_cparams("arbitrary"),
        name="attention_paged",
    )(page_table, qlat3, q3, cnew, pnew, snew_t, ckv_pool, kpe_pool_t, ks_pool_t)


def _uv_pair_kernel(o_ref, w_ref, y_ref):
    r = w_ref.shape[1]
    parts = [_dot(o_ref[:, j * r:(j + 1) * r], w_ref[j]) for j in range(w_ref.shape[0])]
    y_ref[...] = jnp.concatenate(parts, axis=-1).astype(BF16)


def _uv_call(olat2, wuv_h):
    n = olat2.shape[0]
    r = wuv_h.shape[1]
    hpb = LANES // V_DIM
    return pl.pallas_call(
        _uv_pair_kernel,
        grid=(MLA_HEADS // hpb,),
        in_specs=[
            pl.BlockSpec((n, hpb * r), lambda h: (0, h)),
            pl.BlockSpec((hpb, r, V_DIM), lambda h: (h, 0, 0)),
        ],
        out_specs=pl.BlockSpec((n, hpb * V_DIM), lambda h: (0, h)),
        out_shape=jax.ShapeDtypeStruct((n, MLA_HEADS * V_DIM), BF16),
        compiler_params=_cparams("arbitrary"),
        name="value_up",
    )(olat2, wuv_h)


def _outproj_kernel(x_ref, sh_ref, sc_ref, gt_ref, g_ref, wg_ref, ret_ref, att_ref, wpa_ref, wpb_ref, wo_ref, o_ref):
    x = x_ref[...]
    d = x.shape[-1]
    h = _modulated_norm(x, g_ref[...], sh_ref[...], sc_ref[...]).astype(BF16)
    gates = _dot(h, wg_ref[...])
    nv = ret_ref.shape[-1]
    rg, ga, gb = gates[:, :nv], gates[:, nv:nv + d], gates[:, nv + d:]
    ya = _dot((ret_ref[...].astype(F32) * (rg * jax.nn.sigmoid(rg))).astype(BF16), wpa_ref[...])
    yb = _dot(att_ref[...], wpb_ref[...])
    merged = jax.nn.sigmoid(ga) * ya + jax.nn.sigmoid(gb) * yb
    o_ref[...] = x + gt_ref[...] * _dot(merged.astype(BF16), wo_ref[...])


def _outproj_call(x3, mod3, norm_g, w_gates, ret_o, att_o, wpa, wpb, wo, tl):
    bx, lx, d = x3.shape
    lm = mod3.shape[1]
    lm_blk = 1 if lm == 1 else tl

    def tok(n):
        return pl.BlockSpec((None, tl, n), lambda b, i: (b, i, 0))

    return pl.pallas_call(
        _outproj_kernel,
        grid=(bx, lx // tl),
        in_specs=[
            tok(d), _mod_spec(lm_blk, d, 3), _mod_spec(lm_blk, d, 4), _mod_spec(lm_blk, d, 5), _const_spec((1, d)),
            _const_spec(w_gates.shape), tok(ret_o.shape[-1]), tok(att_o.shape[-1]),
            _const_spec(wpa.shape), _const_spec(wpb.shape), _const_spec(wo.shape),
        ],
        out_specs=tok(d),
        out_shape=jax.ShapeDtypeStruct(x3.shape, F32),
        compiler_params=_cparams("arbitrary", "arbitrary"),
        name="mixer_outproj",
    )(x3, mod3, mod3, mod3, norm_g.reshape(1, d), w_gates, ret_o, att_o, wpa, wpb, wo)


def _rope_tables(pos):
    half_r = RET_DK // 2
    fr = ROPE_THETA ** (-jnp.arange(half_r, dtype=F32) / half_r)
    ang = pos[:, None] * fr[None, :]
    cos, sin = jnp.cos(ang), jnp.sin(ang)
    cosr = jnp.concatenate([cos, cos], axis=-1)
    sinr = jnp.concatenate([-sin, sin], axis=-1)
    half_m = ROPE_DIM // 2
    fm = ROPE_THETA ** (-jnp.arange(half_m, dtype=F32) / half_m)
    angm = pos[:, None] * fm[None, :]
    cm, sm = jnp.cos(angm), jnp.sin(angm)
    n = pos.shape[0]
    z16, z32 = jnp.zeros((n, half_m), F32), jnp.zeros((n, LANES - QK_DIM), F32)
    rc = jnp.concatenate([jnp.ones((n, NOPE_DIM), F32), cm, cm, z32], axis=-1)
    rsm = jnp.concatenate([jnp.zeros((n, NOPE_DIM), F32), -sm, z16, z32], axis=-1)
    rsp = jnp.concatenate([jnp.zeros((n, NOPE_DIM), F32), z16, sm, z32], axis=-1)
    return cosr, sinr, rc, rsm, rsp


def _retention_tables(chunk):
    log_gamma = jnp.log1p(-jnp.exp2(-5.0 - jnp.arange(RET_HEADS, dtype=F32)))
    idx = jnp.arange(chunk, dtype=F32)
    rel = idx[:, None] - idx[None, :]
    decay = jnp.where(rel[None] >= 0, jnp.exp(rel[None] * log_gamma[:, None, None]), 0.0)
    xi = jnp.exp((idx[None, :] + 1.0) * log_gamma[:, None])[:, :, None]
    zeta = jnp.exp((chunk - 1.0 - idx)[None, :] * log_gamma[:, None])[:, :, None]
    gch = jnp.exp(chunk * log_gamma)[:, None, None]
    return decay, xi, zeta, gch


def _pad_heads(w, width):
    r, hn, n = w.shape
    return jnp.pad(w, ((0, 0), (0, 0), (0, width - n))).reshape(r, hn * width)


def _prep_layer(l, w_in, w_uq, w_uk, w_uv, w_pa, w_pb, w_out, mla_qa_g, mla_kva_g, mla_qn_g, mla_kn_g):
    d = w_in.shape[1]
    rqk, rvw = RET_HEADS * RET_DK, RET_HEADS * RET_DV
    q_rank, kv_rank = w_uq.shape[1], w_uk.shape[1]
    o = np.cumsum([0, rqk, rqk, rvw, rvw, q_rank, kv_rank, ROPE_DIM, d, d])
    wi = w_in[l]
    rq, rk, rv, rg, cq, ckv, kpe, ga, gb = [wi[:, o[i]:o[i + 1]] for i in range(9)]
    kpe_slot = jnp.pad(kpe, ((0, 0), (NOPE_DIM, LANES - QK_DIM)))
    w_a = jnp.concatenate([rv, rq, rk, cq, ckv, kpe_slot], axis=1).astype(BF16)
    offs = tuple(int(v) for v in np.cumsum([0, rvw, rqk, rqk, q_rank, kv_rank, LANES]))
    w_g = jnp.concatenate([rg, ga, gb], axis=1).astype(BF16)
    wuq = _pad_heads(w_uq[l], HEAD_SLOT).astype(BF16)
    wuk = _pad_heads(w_uk[l], HEAD_SLOT).astype(BF16)
    wuv = w_uv[l].reshape(kv_rank, MLA_HEADS * V_DIM).astype(BF16)
    wukt = jnp.pad(jnp.transpose(w_uk[l], (1, 2, 0)), ((0, 0), (0, HEAD_SLOT - NOPE_DIM), (0, 0))).astype(BF16)
    wuv_h = jnp.transpose(w_uv[l], (1, 0, 2)).astype(BF16)
    gqn, gkn = mla_qn_g[l], mla_kn_g[l]
    zpad = jnp.zeros((LANES - QK_DIM,), F32)
    aq = (jnp.concatenate([gqn[:NOPE_DIM] * gkn[:NOPE_DIM], gqn[NOPE_DIM:], zpad]) * (MLA_SCALE * LOG2E)).reshape(1, LANES)
    ak = jnp.concatenate([jnp.zeros((NOPE_DIM,), F32), gkn[NOPE_DIM:], zpad]).reshape(1, LANES)
    gains = (mla_qa_g[l].reshape(1, q_rank), mla_kva_g[l].reshape(1, kv_rank), aq, ak)
    return dict(w_a=w_a, offs=offs, w_g=w_g, wuq=wuq, wuk=wuk, wuv=wuv, wukt=wukt, wuv_h=wuv_h, gains=gains,
                wpa=w_pa[l].astype(BF16), wpb=w_pb[l].astype(BF16), wo=w_out[l].astype(BF16))


def _largest_tile(n, cap):
    t = min(n, cap)
    while n % t:
        t //= 2
    return t


def kernel(x_prompt, x_sample, cache_ckv, cache_kpe, cache_kscale, state_ret, page_table, c_prompt, c_sample, ada_w, ada_b, norm_g, ffn_w13, ffn_w2, w_in, ret_gn_g, mla_qa_g, mla_kva_g, w_uq, mla_qn_g, mla_kn_g, w_uk, w_uv, w_pa, w_pb, w_out):
    depth = ada_w.shape[0]
    bp, lp, d = x_prompt.shape
    ns, ls, _ = x_sample.shape
    assert ls == 1, "sample group is one new token per sequence"
    assert lp % RET_CHUNK == 0
    n_pages = page_table.shape[1]
    past_len = n_pages * PAGE_SIZE

    mod = _mod_call(jnp.concatenate([c_prompt, c_sample], axis=0), ada_w, ada_b)
    tabs_p = _rope_tables(jnp.arange(lp, dtype=F32))
    tabs_s = _rope_tables(jnp.full((ns,), float(past_len), F32))
    rt_p = _retention_tables(RET_CHUNK)
    rt_s = _retention_tables(1)

    tl_p = _largest_tile(lp, 512)
    tl_in = _largest_tile(lp, 256)
    tq = _largest_tile(lp, 512)
    bb = _largest_tile(ns, 8)
    nbt = _largest_tile(bp, 1)
    rcs = _largest_tile(lp // RET_CHUNK, 4)
    cp = _largest_tile(n_pages // 2, 64)
    kpe_pool_t = jnp.swapaxes(cache_kpe, 2, 3)
    ks_pool_t = jnp.swapaxes(cache_kscale, 2, 3)

    w13_all = ffn_w13.astype(BF16)
    w2_all = ffn_w2.astype(BF16)

    yp = x_prompt
    ys = x_sample.reshape(1, ns, d)
    st_p, st_s = [], []
    ret_s_new = []
    for l in range(depth):
        w = _prep_layer(l, w_in, w_uq, w_uk, w_uv, w_pa, w_pb, w_out, mla_qa_g, mla_kva_g, mla_qn_g, mla_kn_g)
        mod_p = mod[l, :bp].reshape(bp, 1, N_MOD * d)
        mod_s = mod[l, bp:].reshape(1, ns, N_MOD * d)
        ng = norm_g[l]

        yp = _ffn_call(yp, mod_p, 0, ng[0], w13_all, w2_all, l, 0, tl_p)
        rq, rk, rv, q, ckv, kpe, ks, kf, vf = _inproj_call(
            yp, mod_p, ng[1], w["w_a"], w["offs"], tabs_p, w["gains"], w["wuq"], w["wuk"], w["wuv"], tl_in, True)
        ret_o, ret_s = _ret_prompt_call(rq, rk, rv, rt_p, ret_gn_g[l], nbt, rcs)
        att_o = _attn_prompt_call(q, kf, vf, tq)
        yp = _outproj_call(yp, mod_p, ng[1], w["w_g"], ret_o, att_o, w["wpa"], w["wpb"], w["wo"], tl_p)
        yp = _ffn_call(yp, mod_p, 6, ng[2], w13_all, w2_all, l, 1, tl_p)
        st_p.append((ckv, kpe, ks, ret_s))

        ys = _ffn_call(ys, mod_s, 0, ng[0], w13_all, w2_all, l, 0, ns)
        rq, rk, rv, q, ckv, kpe, ks = _inproj_call(
            ys, mod_s, ng[1], w["w_a"], w["offs"], tabs_s, w["gains"], w["wuq"], w["wuk"], w["wuv"], ns, False)
        last = l == depth - 1
        ret_o, ret_s = _ret_sample_call(rq[0], rk[0], rv[0], state_ret, ret_s_new if last else [], l, rt_s,
                                        ret_gn_g[l], bb)
        ret_s_new.append(ret_s)
        qlat = _qlat_call(q[0], w["wukt"])

        olat = _attn_paged_call(page_table, qlat.reshape(ns, MLA_HEADS, -1), q[0].reshape(ns, MLA_HEADS, HEAD_SLOT),
                                ckv.reshape(ns, 1, -1), kpe.reshape(ns, 1, -1), ks.reshape(ns, MLA_HEADS, 1),
                                cache_ckv, kpe_pool_t, ks_pool_t, l, cp)
        att_o = _uv_call(olat.reshape(ns, -1), w["wuv_h"])
        ys = _outproj_call(ys, mod_s, ng[1], w["w_g"], ret_o[None], att_o[None], w["wpa"], w["wpb"], w["wo"], ns)
        ys = _ffn_call(ys, mod_s, 6, ng[2], w13_all, w2_all, l, 1, ns)
        st_s.append((ckv.reshape(ns, 1, -1), kpe.reshape(ns, 1, -1), ks.reshape(ns, 1, -1)))

    def stack(states, i):
        return jnp.stack([st[i] for st in states], axis=0)

    return (yp, ys.reshape(ns, 1, d),
            stack(st_p, 0), stack(st_p, 1), stack(st_p, 2), stack(st_p, 3),
            stack(st_s, 0), stack(st_s, 1), stack(st_s, 2),
            ret_s_new[-1] if depth > 1 else ret_s_new[-1][None])
```

```python
import functools

import jax
import jax.numpy as jnp
import numpy as np
from jax import lax
from jax.experimental import pallas as pl
from jax.experimental.pallas import tpu as pltpu

F32 = jnp.float32
BF16 = jnp.bfloat16

RET_HEADS = 4
RET_DK = 128
RET_DV = 256
RET_CHUNK = 128
MLA_HEADS = 16
NOPE_DIM = 64
ROPE_DIM = 32
V_DIM = 64
QK_DIM = NOPE_DIM + ROPE_DIM
MLA_SCALE = QK_DIM ** -0.5
LOG2E = float(np.log2(np.e))
ROPE_THETA = 10000.0
N_MOD = 9
EPS = 1e-6
PAGE_SIZE = 128

LANES = 128
MXU_DEPTH = 256
HEAD_SLOT = LANES
VMEM_LIMIT = 56 * 1024 * 1024
NEG_INF = float("-inf")


def _cparams(*sem):
    return pltpu.CompilerParams(dimension_semantics=sem, vmem_limit_bytes=VMEM_LIMIT)


def _dot(a, b):
    return jnp.dot(a, b, preferred_element_type=F32)


def _dot_nt(a, b):
    return lax.dot_general(a, b, (((1,), (1,)), ((), ())), preferred_element_type=F32)


def _dot_tn(a, b):
    return lax.dot_general(a, b, (((0,), (0,)), ((), ())), preferred_element_type=F32)


def _const_spec(shape):
    n = len(shape)
    return pl.BlockSpec(shape, lambda *_: (0,) * n)


def _modulated_norm(x, g, shift, scale):
    ms = jnp.mean(x * x, axis=-1, keepdims=True)
    y = x * lax.rsqrt(ms + EPS) * g
    return y * (1.0 + scale) + shift


def _mod_spec(lm_blk, d, k):
    if lm_blk == 1:
        return pl.BlockSpec((None, 1, d), lambda b, i: (b, 0, k))
    return pl.BlockSpec((None, lm_blk, d), lambda b, i: (b, i, k))


def _mod_kernel(c_ref, w_ref, b_ref, o_ref):
    c = c_ref[...]
    s = (c * jax.nn.sigmoid(c)).astype(BF16)
    o_ref[...] = _dot(s, w_ref[...].astype(BF16)) + b_ref[...]


def _mod_call(c_all, ada_w, ada_b):
    depth, d, nd = ada_w.shape
    n = c_all.shape[0]
    return pl.pallas_call(
        _mod_kernel,
        grid=(depth, nd // d),
        in_specs=[
            pl.BlockSpec((n, d), lambda l, j: (0, 0)),
            pl.BlockSpec((None, d, d), lambda l, j: (l, 0, j)),
            pl.BlockSpec((None, 1, d), lambda l, j: (l, 0, j)),
        ],
        out_specs=pl.BlockSpec((None, n, d), lambda l, j: (l, 0, j)),
        out_shape=jax.ShapeDtypeStruct((depth, n, nd), F32),
        compiler_params=_cparams("arbitrary", "arbitrary"),
        name="adaln_mod",
    )(c_all, ada_w, ada_b.reshape(depth, 1, nd))


def _ffn_kernel(x_ref, sh_ref, sc_ref, gt_ref, g_ref, w13_ref, w2_ref, o_ref, *, d_ff):
    x = x_ref[...]
    h = _modulated_norm(x, g_ref[...], sh_ref[...], sc_ref[...]).astype(BF16)
    ab = _dot(h, w13_ref[...])
    a, b = ab[:, :d_ff], ab[:, d_ff:]
    u = (a * jax.nn.sigmoid(a) * b).astype(BF16)
    y = _dot(u, w2_ref[...])
    o_ref[...] = x + 0.5 * gt_ref[...] * y


def _ffn_call(x3, mod3, k0, norm_g, w13_all, w2_all, layer, j, tl):
    bx, lx, d = x3.shape
    lm = mod3.shape[1]
    lm_blk = 1 if lm == 1 else tl
    d_ff = w2_all.shape[2]
    return pl.pallas_call(
        functools.partial(_ffn_kernel, d_ff=d_ff),
        grid=(bx, lx // tl),
        in_specs=[
            pl.BlockSpec((None, tl, d), lambda b, i: (b, i, 0)),
            _mod_spec(lm_blk, d, k0), _mod_spec(lm_blk, d, k0 + 1), _mod_spec(lm_blk, d, k0 + 2),
            _const_spec((1, d)),
            pl.BlockSpec((None, None) + w13_all.shape[2:], lambda b, i: (layer, j, 0, 0), pipeline_mode=pl.Buffered(1)),
            pl.BlockSpec((None, None) + w2_all.shape[2:], lambda b, i: (layer, j, 0, 0), pipeline_mode=pl.Buffered(1)),
        ],
        out_specs=pl.BlockSpec((None, tl, d), lambda b, i: (b, i, 0)),
        out_shape=jax.ShapeDtypeStruct(x3.shape, F32),
        compiler_params=_cparams("arbitrary", "arbitrary"),
        name="ffn",
    )(x3, mod3, mod3, mod3, norm_g.reshape(1, d), w13_all, w2_all)


def _rope_slot(t, c, sm, sp):
    return t * c + pltpu.roll(t, LANES - ROPE_DIM // 2, axis=1) * sm + pltpu.roll(t, ROPE_DIM // 2, axis=1) * sp


def _slot_sumsq(v, ones_ref):
    w = ones_ref.shape[0]
    ones_bd = ones_ref[...]
    return jnp.concatenate([_dot((v[:, j:j + w] * v[:, j:j + w]).astype(BF16), ones_bd)
                            for j in range(0, v.shape[1], w)], axis=1)


def _inproj_kernel(x_ref, sh_ref, sc_ref, g_ref, w_ref, cosr_ref, sinr_ref, rc_ref, rsm_ref, rsp_ref,
                   gqa_ref, gkva_ref, aq_ref, ak_ref, wuq_ref, wuk_ref, wuv_ref, ones_ref,
                   rq_ref, rk_ref, rv_ref, q_ref, ckv_ref, kpe_ref, ks_ref, *kv_refs, offs):
    _inproj_rows(slice(0, x_ref.shape[0]), x_ref, sh_ref, sc_ref, g_ref, w_ref, cosr_ref, sinr_ref,
                 rc_ref, rsm_ref, rsp_ref, gqa_ref, gkva_ref, aq_ref, ak_ref, wuq_ref, wuk_ref, wuv_ref, ones_ref,
                 rq_ref, rk_ref, rv_ref, q_ref, ckv_ref, kpe_ref, ks_ref, kv_refs, offs)


def _inproj_rows(rows, x_ref, sh_ref, sc_ref, g_ref, w_ref, cosr_ref, sinr_ref, rc_ref, rsm_ref, rsp_ref,
                 gqa_ref, gkva_ref, aq_ref, ak_ref, wuq_ref, wuk_ref, wuv_ref, ones_ref,
                 rq_ref, rk_ref, rv_ref, q_ref, ckv_ref, kpe_ref, ks_ref, kv_refs, offs):
    o_rv, o_rq, o_rk, o_cq, o_ckv, o_kpe, n_in = offs

    def mod_rows(ref):
        return ref[...] if ref.shape[0] == 1 else ref[rows, :]

    x = x_ref[rows, :]
    h = _modulated_norm(x, g_ref[...], mod_rows(sh_ref), mod_rows(sc_ref)).astype(BF16)
    p = _dot(h, w_ref[...])

    rv_ref[rows, :] = p[:, o_rv:o_rq].astype(BF16)
    cosr, sinr = cosr_ref[rows, :], sinr_ref[rows, :]
    for hh in range(RET_HEADS):
        lo, hi = hh * RET_DK, (hh + 1) * RET_DK
        tq = p[:, o_rq + lo:o_rq + hi]
        rq_ref[rows, lo:hi] = (tq * cosr + pltpu.roll(tq, RET_DK // 2, axis=1) * sinr).astype(BF16)
        tk = p[:, o_rk + lo:o_rk + hi]
        rk_ref[rows, lo:hi] = ((tk * cosr + pltpu.roll(tk, RET_DK // 2, axis=1) * sinr) * (RET_DK ** -0.5)).astype(BF16)

    rc, rsm, rsp = rc_ref[rows, :], rsm_ref[rows, :], rsp_ref[rows, :]

    cq = p[:, o_cq:o_ckv]
    cq = (cq * lax.rsqrt(jnp.mean(cq * cq, axis=-1, keepdims=True) + EPS) * gqa_ref[...]).astype(BF16)
    qf = _dot(cq, wuq_ref[...])
    qn = qf * lax.rsqrt(_slot_sumsq(qf, ones_ref) * (1.0 / QK_DIM) + EPS) * aq_ref[...]
    for hh in range(MLA_HEADS):
        lo, hi = hh * HEAD_SLOT, (hh + 1) * HEAD_SLOT
        q_ref[rows, lo:hi] = _rope_slot(qn[:, lo:hi], rc, rsm, rsp).astype(BF16)

    craw = p[:, o_ckv:o_kpe]
    ckv = craw * lax.rsqrt(jnp.mean(craw * craw, axis=-1, keepdims=True) + EPS) * gkva_ref[...]
    ckv_ref[rows, :] = ckv
    ckv_bf = ckv.astype(BF16)
    kn = _dot(ckv_bf, wuk_ref[...])
    kblk = p[:, o_kpe:n_in]
    kpe_ss = _dot((kblk * kblk).astype(BF16), ones_ref[:HEAD_SLOT, :HEAD_SLOT])
    kpe_rot = _rope_slot(kblk * ak_ref[...], rc, rsm, rsp)
    kpe_ref[rows, :] = kpe_rot[:, NOPE_DIM:QK_DIM]
    ms = (_slot_sumsq(kn, ones_ref) + _lane_tile(kpe_ss, MLA_HEADS)) * (1.0 / QK_DIM)
    ks_rep = lax.rsqrt(ms + EPS)
    lane16 = lax.broadcasted_iota(jnp.int32, (x.shape[0], MLA_HEADS), 1)
    ks_all = jnp.zeros((x.shape[0], MLA_HEADS), F32)
    for hh in range(MLA_HEADS):
        ks_all = jnp.where(lane16 == hh, ks_rep[:, hh * HEAD_SLOT:hh * HEAD_SLOT + MLA_HEADS], ks_all)
    ks_ref[rows, :] = ks_all
    if kv_refs:
        kv_refs[0][rows, :] = ((kn + _lane_tile(kpe_rot, MLA_HEADS)) * ks_rep).astype(BF16)
        kv_refs[1][rows, :] = _dot(ckv_bf, wuv_ref[...]).astype(BF16)


def _inproj_call(x3, mod3, norm_g, w_in_a, offs, tabs, gains, wuq, wuk, wuv, tl, emit_kv):
    bx, lx, d = x3.shape
    lm = mod3.shape[1]
    lm_blk = 1 if lm == 1 else tl
    cosr, sinr, rc, rsm, rsp = tabs
    gqa, gkva, aq, ak = gains
    hs = MLA_HEADS * HEAD_SLOT
    aq = jnp.tile(aq, (1, MLA_HEADS))
    slot_id = jnp.arange(MXU_DEPTH) // HEAD_SLOT
    ones_bd = (slot_id[:, None] == slot_id[None, :]).astype(BF16)

    def tok(n):
        return pl.BlockSpec((None, tl, n), lambda b, i: (b, i, 0))

    def tab():
        return pl.BlockSpec((tl, LANES), lambda b, i: (i, 0))

    def out(n, dt):
        return jax.ShapeDtypeStruct((bx, lx, n), dt)

    out_specs = [tok(RET_HEADS * RET_DK), tok(RET_HEADS * RET_DK), tok(RET_HEADS * RET_DV), tok(hs),
                 tok(wuk.shape[0]), tok(ROPE_DIM), tok(MLA_HEADS)]
    out_shape = [out(RET_HEADS * RET_DK, BF16), out(RET_HEADS * RET_DK, BF16), out(RET_HEADS * RET_DV, BF16),
                 out(hs, BF16), out(wuk.shape[0], F32), out(ROPE_DIM, F32), out(MLA_HEADS, F32)]
    if emit_kv:
        out_specs += [tok(hs), tok(MLA_HEADS * V_DIM)]
        out_shape += [out(hs, BF16), out(MLA_HEADS * V_DIM, BF16)]
    return pl.pallas_call(
        functools.partial(_inproj_kernel, offs=offs),
        grid=(bx, lx // tl),
        in_specs=[
            tok(d), _mod_spec(lm_blk, d, 3), _mod_spec(lm_blk, d, 4), _const_spec((1, d)),
            _const_spec(w_in_a.shape), tab(), tab(), tab(), tab(), tab(),
            _const_spec(gqa.shape), _const_spec(gkva.shape), _const_spec(aq.shape), _const_spec(ak.shape),
            _const_spec(wuq.shape), _const_spec(wuk.shape), _const_spec(wuv.shape), _const_spec(ones_bd.shape),
        ],
        out_specs=out_specs,
        out_shape=out_shape,
        compiler_params=_cparams("arbitrary", "arbitrary"),
        name="mixer_inproj",
    )(x3, mod3, mod3, norm_g.reshape(1, d), w_in_a, cosr, sinr, rc, rsm, rsp, gqa, gkva, aq, ak, wuq, wuk, wuv,
      ones_bd)


def _group_norm(o, g):
    mu = jnp.mean(o, axis=-1, keepdims=True)
    oc = o - mu
    var = jnp.mean(oc * oc, axis=-1, keepdims=True)
    return oc * lax.rsqrt(var + EPS) * g


def _ret_prompt_kernel(q_ref, k_ref, v_ref, dec_ref, xi_ref, zeta_ref, gch_ref, gn_ref, o_ref, sout_ref, s_scr):
    c = pl.program_id(1)

    @pl.when(c == 0)
    def _():
        s_scr[...] = jnp.zeros_like(s_scr)

    ch = dec_ref.shape[1]
    for bi in range(q_ref.shape[0]):
        for hh in range(RET_HEADS):
            s = s_scr[bi, hh]
            for ci in range(q_ref.shape[1] // ch):
                rows = slice(ci * ch, (ci + 1) * ch)
                q = q_ref[bi, rows, hh * RET_DK:(hh + 1) * RET_DK]
                k = k_ref[bi, rows, hh * RET_DK:(hh + 1) * RET_DK]
                v = v_ref[bi, rows, hh * RET_DV:(hh + 1) * RET_DV]
                scores = _dot_nt(q, k) * dec_ref[hh]
                inner = _dot(scores.astype(BF16), v)
                cross = _dot(q, s.astype(BF16)) * xi_ref[hh]
                kz = (k.astype(F32) * zeta_ref[hh]).astype(BF16)
                s = gch_ref[hh] * s + _dot_tn(kz, v)
                o_ref[bi, rows, hh * RET_DV:(hh + 1) * RET_DV] = _group_norm(inner + cross, gn_ref[hh]).astype(BF16)
            s_scr[bi, hh] = s

    @pl.when(c == pl.num_programs(1) - 1)
    def _():
        sout_ref[...] = s_scr[...]


def _ret_prompt_call(rq, rk, rv, ret_tabs, gn_g, nbt, chunks_per_step):
    bx, lx, _ = rq.shape
    ch = RET_CHUNK * chunks_per_step
    dec, xi, zeta, gch = ret_tabs
    return pl.pallas_call(
        _ret_prompt_kernel,
        grid=(bx // nbt, lx // ch),
        in_specs=[
            pl.BlockSpec((nbt, ch, RET_HEADS * RET_DK), lambda b, c: (b, c, 0)),
            pl.BlockSpec((nbt, ch, RET_HEADS * RET_DK), lambda b, c: (b, c, 0)),
            pl.BlockSpec((nbt, ch, RET_HEADS * RET_DV), lambda b, c: (b, c, 0)),
            _const_spec(dec.shape), _const_spec(xi.shape), _const_spec(zeta.shape), _const_spec(gch.shape),
            _const_spec((RET_HEADS, 1, RET_DV)),
        ],
        out_specs=[
            pl.BlockSpec((nbt, ch, RET_HEADS * RET_DV), lambda b, c: (b, c, 0)),
            pl.BlockSpec((nbt, RET_HEADS, RET_DK, RET_DV), lambda b, c: (b, 0, 0, 0)),
        ],
        out_shape=[
            jax.ShapeDtypeStruct((bx, lx, RET_HEADS * RET_DV), BF16),
            jax.ShapeDtypeStruct((bx, RET_HEADS, RET_DK, RET_DV), F32),
        ],
        scratch_shapes=[pltpu.VMEM((nbt, RET_HEADS, RET_DK, RET_DV), F32)],
        compiler_params=_cparams("arbitrary", "arbitrary"),
        name="retention_prompt",
    )(rq, rk, rv, dec, xi, zeta, gch, gn_g.reshape(RET_HEADS, 1, RET_DV))


def _ret_sample_kernel(qt_ref, kt_ref, v_ref, s_ref, dec_ref, xi_ref, zeta_ref, gch_ref, gn_ref, *rest, bb):
    prev_refs, (o_ref, sout_ref, pre_scr) = rest[:-3], rest[-3:]
    for li, prev_ref in enumerate(prev_refs):
        sout_ref[li] = prev_ref[...]
    own = sout_ref.at[len(prev_refs)] if prev_refs else sout_ref
    for j in range(bb):
        for hh in range(RET_HEADS):
            qcol = qt_ref[hh * RET_DK:(hh + 1) * RET_DK, j:j + 1].astype(F32)
            kcol = kt_ref[hh * RET_DK:(hh + 1) * RET_DK, j:j + 1].astype(F32)
            vrow = v_ref[j:j + 1, hh * RET_DV:(hh + 1) * RET_DV].astype(F32)
            s0 = s_ref[j, hh]
            score = jnp.sum(qcol * kcol, axis=0, keepdims=True) * dec_ref[hh]
            inner = score * vrow
            cross = jnp.sum(qcol * s0, axis=0, keepdims=True) * xi_ref[hh]
            own[j, hh] = gch_ref[hh] * s0 + (kcol * zeta_ref[hh]) * vrow
            pre_scr[j:j + 1, hh * RET_DV:(hh + 1) * RET_DV] = inner + cross
    for hh in range(RET_HEADS):
        cols = slice(hh * RET_DV, (hh + 1) * RET_DV)
        o_ref[:, cols] = _group_norm(pre_scr[:, cols], gn_ref[hh]).astype(BF16)


def _ret_sample_call(rq, rk, rv, state_all, prev_new, layer, ret_tabs, gn_g, bb):
    n = rq.shape[0]
    dec, xi, zeta, gch = ret_tabs
    dqk = RET_HEADS * RET_DK
    qt = rq.reshape(n // bb, bb, dqk).transpose(0, 2, 1)
    kt = rk.reshape(n // bb, bb, dqk).transpose(0, 2, 1)
    st_blk = (bb, RET_HEADS, RET_DK, RET_DV)
    if prev_new:
        n_out = len(prev_new) + 1
        st_spec = pl.BlockSpec((n_out,) + st_blk, lambda i: (0, i, 0, 0, 0))
        st_shape = jax.ShapeDtypeStruct((n_out,) + state_all.shape[1:], F32)
    else:
        st_spec = pl.BlockSpec(st_blk, lambda i: (i, 0, 0, 0))
        st_shape = jax.ShapeDtypeStruct(state_all.shape[1:], F32)
    return pl.pallas_call(
        functools.partial(_ret_sample_kernel, bb=bb),
        grid=(n // bb,),
        in_specs=[
            pl.BlockSpec((None, dqk, bb), lambda i: (i, 0, 0)),
            pl.BlockSpec((None, dqk, bb), lambda i: (i, 0, 0)),
            pl.BlockSpec((bb, RET_HEADS * RET_DV), lambda i: (i, 0)),
            pl.BlockSpec((None,) + st_blk, lambda i: (layer, i, 0, 0, 0)),
            _const_spec(dec.shape), _const_spec(xi.shape), _const_spec(zeta.shape), _const_spec(gch.shape),
            _const_spec((RET_HEADS, 1, RET_DV)),
        ] + [pl.BlockSpec(st_blk, lambda i: (i, 0, 0, 0)) for _ in prev_new],
        out_specs=[pl.BlockSpec((bb, RET_HEADS * RET_DV), lambda i: (i, 0)), st_spec],
        out_shape=[jax.ShapeDtypeStruct((n, RET_HEADS * RET_DV), BF16), st_shape],
        scratch_shapes=[pltpu.VMEM((bb, RET_HEADS * RET_DV), F32)],
        compiler_params=_cparams("arbitrary"),
        name="retention_sample",
    )(qt, kt, rv, state_all, dec, xi, zeta, gch, gn_g.reshape(RET_HEADS, 1, RET_DV), *prev_new)


HEADS_PER_GROUP = 4


def _lane_tile(x, n):
    return x if n == 1 else jnp.concatenate([x] * n, axis=1)


def _attn_prompt_kernel(q_ref, k_ref, v_ref, o_ref, *, tq, nq):
    i = pl.program_id(2)
    half = tq // 2 if (tq // 2) % LANES == 0 else tq
    vw = o_ref.shape[-1]
    vrep = vw // LANES
    row = lax.broadcasted_iota(jnp.int32, (half, half), 0)
    col = lax.broadcasted_iota(jnp.int32, (half, half), 1)
    tri = jnp.where(row >= col, 0.0, NEG_INF).astype(F32)
    lane = lax.broadcasted_iota(jnp.int32, (half, vw), 1)

    def rows_block(r0, n):
        rows = slice(r0, r0 + half)
        vblk = v_ref[0:n, :]
        o = jnp.zeros((half, vw), F32)
        for hh in range(HEADS_PER_GROUP):
            s = _dot_nt(q_ref[rows, hh * HEAD_SLOT:(hh + 1) * HEAD_SLOT], k_ref[0:n, hh * HEAD_SLOT:(hh + 1) * HEAD_SLOT])
            s = jnp.concatenate([s[:, :n - half], s[:, n - half:] + tri], axis=1) if n > half else s + tri
            m = jnp.broadcast_to(jnp.max(s, axis=-1, keepdims=True), (half, LANES))
            p = jnp.exp2(s - _lane_tile(m, n // LANES))
            l = jnp.broadcast_to(jnp.sum(p, axis=-1, keepdims=True), (half, LANES))
            oh = _dot(p.astype(BF16), vblk) * _lane_tile(1.0 / l, vrep)
            o = jnp.where((lane >= hh * V_DIM) & (lane < (hh + 1) * V_DIM), oh, o)
        o_ref[rows, :] = o.astype(BF16)

    for j in range(nq):
        @pl.when(i == j)
        def _():
            for r0 in range(0, tq, half):
                rows_block(r0, j * tq + r0 + half)


def _attn_prompt_call(q, k, v, tq):
    bx, lx, _ = q.shape
    gw = HEADS_PER_GROUP * HEAD_SLOT
    vw = HEADS_PER_GROUP * V_DIM
    ng = MLA_HEADS // HEADS_PER_GROUP
    return pl.pallas_call(
        functools.partial(_attn_prompt_kernel, tq=tq, nq=lx // tq),
        grid=(bx, ng, lx // tq),
        in_specs=[
            pl.BlockSpec((None, tq, gw), lambda b, g, i: (b, i, g)),
            pl.BlockSpec((None, lx, gw), lambda b, g, i: (b, 0, g)),
            pl.BlockSpec((None, lx, vw), lambda b, g, i: (b, 0, g)),
        ],
        out_specs=pl.BlockSpec((None, tq, vw), lambda b, g, i: (b, i, g)),
        out_shape=jax.ShapeDtypeStruct((bx, lx, MLA_HEADS * V_DIM), BF16),
        compiler_params=_cparams("arbitrary", "arbitrary", "arbitrary"),
        name="attention_prompt",
    )(q, k, v)


def _qlat_kernel(q_ref, w_ref, o_ref):
    o_ref[...] = _dot(q_ref[...], w_ref[...]).astype(BF16)


def _qlat_call(q2, wukt):
    n = q2.shape[0]
    r = wukt.shape[2]
    return pl.pallas_call(
        _qlat_kernel,
        grid=(MLA_HEADS,),
        in_specs=[
            pl.BlockSpec((n, HEAD_SLOT), lambda h: (0, h)),
            pl.BlockSpec((None, HEAD_SLOT, r), lambda h: (h, 0, 0)),
        ],
        out_specs=pl.BlockSpec((n, r), lambda h: (0, h)),
        out_shape=jax.ShapeDtypeStruct((n, MLA_HEADS * r), BF16),
        compiler_params=_cparams("arbitrary"),
        name="absorb_q",
    )(q2, wukt)


PV_SPLIT = 4


def _attn_paged_kernel(pt_ref, qlat_ref, q_ref, cnew_ref, pnew_ref, snew_ref, ckv_hbm, kpe_hbm, ks_hbm, o_ref,
                       ckv_buf, kpe_buf, ks_buf, sem, *, layer, cp, n_chunk):
    n_slots = ckv_buf.shape[0]
    b = pl.program_id(0)
    nb = pl.num_programs(0)

    def page_copies(bi, ci, slot, g):
        page = pt_ref[bi, ci * cp + g]
        lanes = pl.ds(pl.multiple_of(g * PAGE_SIZE, PAGE_SIZE), PAGE_SIZE)
        return (pltpu.make_async_copy(ckv_hbm.at[layer, page], ckv_buf.at[slot, g], sem.at[slot, 0]),
                pltpu.make_async_copy(kpe_hbm.at[layer, page], kpe_buf.at[slot, :, lanes], sem.at[slot, 1]),
                pltpu.make_async_copy(ks_hbm.at[layer, page], ks_buf.at[slot, :, lanes], sem.at[slot, 2]))

    def start(bi, ci, slot):
        def one(g, carry):
            for c in page_copies(bi, ci, slot, g):
                c.start()
            return carry
        lax.fori_loop(0, cp, one, 0, unroll=4)

    def start_inline(bi, ci, slot):
        for g in range(cp):
            for c in page_copies(bi, ci, slot, g):
                c.start(priority=g % 2)

    def wait(slot):
        for j, buf in enumerate((ckv_buf, kpe_buf, ks_buf)):
            pltpu.make_async_copy(buf.at[slot], buf.at[slot], sem.at[slot, j]).wait()

    @pl.when(b == 0)
    def _():
        for ci in range(n_chunk):
            start(0, ci, ci)

    qlat = qlat_ref[...]
    qpe = q_ref[:, NOPE_DIM:QK_DIM]
    h, r = qlat.shape
    m = jnp.full((h, 1), NEG_INF, F32)
    l = jnp.zeros((h, 1), F32)
    accs = [jnp.zeros((h, r), F32) for _ in range(PV_SPLIT)]
    kk = cp * PAGE_SIZE // PV_SPLIT

    def chunk(ci, carry):
        m, l, accs = carry
        slot = lax.rem(b * n_chunk + ci, n_slots)
        wait(slot)
        ckv_bf = ckv_buf[slot].reshape(cp * PAGE_SIZE, r).astype(BF16)
        s_lat = jnp.concatenate([_dot_nt(qlat, ckv_bf[j * kk:(j + 1) * kk]) for j in range(PV_SPLIT)], axis=1)
        s = (s_lat + _dot(qpe, kpe_buf[slot].astype(BF16))) * ks_buf[slot]
        m_new = jnp.maximum(m, jnp.max(s, axis=-1, keepdims=True))
        alpha = jnp.exp2(m - m_new)
        p = jnp.exp2(s - m_new)
        l = alpha * l + jnp.sum(p, axis=-1, keepdims=True)
        p_bf = p.astype(BF16)
        accs = [alpha * a + _dot(p_bf[:, j * kk:(j + 1) * kk], ckv_bf[j * kk:(j + 1) * kk]) for j, a in enumerate(accs)]

        start_inline(lax.rem(b + 1, nb), ci, lax.rem((b + 1) * n_chunk + ci, n_slots))
        return m_new, l, accs

    m, l, accs = lax.fori_loop(0, n_chunk, chunk, (m, l, accs))

    @pl.when(b == nb - 1)
    def _():
        for ci in range(n_chunk):
            wait(lax.rem(nb * n_chunk + ci, n_slots))

    cn = cnew_ref[...]
    s_new = (jnp.sum(qlat.astype(F32) * cn, axis=-1, keepdims=True)
             + jnp.sum(qpe.astype(F32) * pnew_ref[...], axis=-1, keepdims=True)) * snew_ref[...]
    m_new = jnp.maximum(m, s_new)
    alpha = jnp.exp2(m - m_new)
    p_new = jnp.exp2(s_new - m_new)
    l = alpha * l + p_new
    acc = alpha * functools.reduce(lambda x, y: x + y, accs) + p_new * cn
    o_ref[...] = (acc / l).astype(BF16)


def _attn_paged_call(page_table, qlat3, q3, cnew, pnew, snew_t, ckv_pool, kpe_pool_t, ks_pool_t, layer, cp):
    nb, n_pages = page_table.shape
    assert n_pages % cp == 0
    n_chunk = n_pages // cp
    n_slots = n_chunk + 1
    r = ckv_pool.shape[-1]
    grid_spec = pltpu.PrefetchScalarGridSpec(
        num_scalar_prefetch=1,
        grid=(nb,),
        in_specs=[
            pl.BlockSpec((None, MLA_HEADS, r), lambda b, pt: (b, 0, 0)),
            pl.BlockSpec((None, MLA_HEADS, HEAD_SLOT), lambda b, pt: (b, 0, 0)),
            pl.BlockSpec((None, 1, r), lambda b, pt: (b, 0, 0)),
            pl.BlockSpec((None, 1, ROPE_DIM), lambda b, pt: (b, 0, 0)),
            pl.BlockSpec((None, MLA_HEADS, 1), lambda b, pt: (b, 0, 0)),
            pl.BlockSpec(memory_space=pl.ANY),
            pl.BlockSpec(memory_space=pl.ANY),
            pl.BlockSpec(memory_space=pl.ANY),
        ],
        out_specs=pl.BlockSpec((None, MLA_HEADS, r), lambda b, pt: (b, 0, 0)),
        scratch_shapes=[
            pltpu.VMEM((n_slots, cp, PAGE_SIZE, r), F32),
            pltpu.VMEM((n_slots, ROPE_DIM, cp * PAGE_SIZE), F32),
            pltpu.VMEM((n_slots, MLA_HEADS, cp * PAGE_SIZE), F32),
            pltpu.SemaphoreType.DMA((n_slots, 3)),
        ],
    )
    return pl.pallas_call(
        functools.partial(_attn_paged_kernel, layer=layer, cp=cp, n_chunk=n_chunk),
        grid_spec=grid_spec,
        out_shape=jax.ShapeDtypeStruct((nb, MLA_HEADS, r), BF16),
        compiler_params=_cparams("arbitrary"),
        name="attention_paged",
    )(page_table, qlat3, q3, cnew, pnew, snew_t, ckv_pool, kpe_pool_t, ks_pool_t)


def _uv_pair_kernel(o_ref, w_ref, y_ref):
    r = w_ref.shape[1]
    parts = [_dot(o_ref[:, j * r:(j + 1) * r], w_ref[j]) for j in range(w_ref.shape[0])]
    y_ref[...] = jnp.concatenate(parts, axis=-1).astype(BF16)


def _uv_call(olat2, wuv_h):
    n = olat2.shape[0]
    r = wuv_h.shape[1]
    hpb = LANES // V_DIM
    return pl.pallas_call(
        _uv_pair_kernel,
        grid=(MLA_HEADS // hpb,),
        in_specs=[
            pl.BlockSpec((n, hpb * r), lambda h: (0, h)),
            pl.BlockSpec((hpb, r, V_DIM), lambda h: (h, 0, 0)),
        ],
        out_specs=pl.BlockSpec((n, hpb * V_DIM), lambda h: (0, h)),
        out_shape=jax.ShapeDtypeStruct((n, MLA_HEADS * V_DIM), BF16),
        compiler_params=_cparams("arbitrary"),
        name="value_up",
    )(olat2, wuv_h)


def _outproj_kernel(x_ref, sh_ref, sc_ref, gt_ref, g_ref, wg_ref, ret_ref, att_ref, wpa_ref, wpb_ref, wo_ref, o_ref):
    x = x_ref[...]
    d = x.shape[-1]
    h = _modulated_norm(x, g_ref[...], sh_ref[...], sc_ref[...]).astype(BF16)
    gates = _dot(h, wg_ref[...])
    nv = ret_ref.shape[-1]
    rg, ga, gb = gates[:, :nv], gates[:, nv:nv + d], gates[:, nv + d:]
    ya = _dot((ret_ref[...].astype(F32) * (rg * jax.nn.sigmoid(rg))).astype(BF16), wpa_ref[...])
    yb = _dot(att_ref[...], wpb_ref[...])
    merged = jax.nn.sigmoid(ga) * ya + jax.nn.sigmoid(gb) * yb
    o_ref[...] = x + gt_ref[...] * _dot(merged.astype(BF16), wo_ref[...])


def _outproj_call(x3, mod3, norm_g, w_gates, ret_o, att_o, wpa, wpb, wo, tl):
    bx, lx, d = x3.shape
    lm = mod3.shape[1]
    lm_blk = 1 if lm == 1 else tl

    def tok(n):
        return pl.BlockSpec((None, tl, n), lambda b, i: (b, i, 0))

    return pl.pallas_call(
        _outproj_kernel,
        grid=(bx, lx // tl),
        in_specs=[
            tok(d), _mod_spec(lm_blk, d, 3), _mod_spec(lm_blk, d, 4), _mod_spec(lm_blk, d, 5), _const_spec((1, d)),
            _const_spec(w_gates.shape), tok(ret_o.shape[-1]), tok(att_o.shape[-1]),
            _const_spec(wpa.shape), _const_spec(wpb.shape), _const_spec(wo.shape),
        ],
        out_specs=tok(d),
        out_shape=jax.ShapeDtypeStruct(x3.shape, F32),
        compiler_params=_cparams("arbitrary", "arbitrary"),
        name="mixer_outproj",
    )(x3, mod3, mod3, mod3, norm_g.reshape(1, d), w_gates, ret_o, att_o, wpa, wpb, wo)


def _rope_tables(pos):
    half_r = RET_DK // 2
    fr = ROPE_THETA ** (-jnp.arange(half_r, dtype=F32) / half_r)
    ang = pos[:, None] * fr[None, :]
    cos, sin = jnp.cos(ang), jnp.sin(ang)
    cosr = jnp.concatenate([cos, cos], axis=-1)
    sinr = jnp.concatenate([-sin, sin], axis=-1)
    half_m = ROPE_DIM // 2
    fm = ROPE_THETA ** (-jnp.arange(half_m, dtype=F32) / half_m)
    angm = pos[:, None] * fm[None, :]
    cm, sm = jnp.cos(angm), jnp.sin(angm)
    n = pos.shape[0]
    z16, z32 = jnp.zeros((n, half_m), F32), jnp.zeros((n, LANES - QK_DIM), F32)
    rc = jnp.concatenate([jnp.ones((n, NOPE_DIM), F32), cm, cm, z32], axis=-1)
    rsm = jnp.concatenate([jnp.zeros((n, NOPE_DIM), F32), -sm, z16, z32], axis=-1)
    rsp = jnp.concatenate([jnp.zeros((n, NOPE_DIM), F32), z16, sm, z32], axis=-1)
    return cosr, sinr, rc, rsm, rsp


def _retention_tables(chunk):
    log_gamma = jnp.log1p(-jnp.exp2(-5.0 - jnp.arange(RET_HEADS, dtype=F32)))
    idx = jnp.arange(chunk, dtype=F32)
    rel = idx[:, None] - idx[None, :]
    decay = jnp.where(rel[None] >= 0, jnp.exp(rel[None] * log_gamma[:, None, None]), 0.0)
    xi = jnp.exp((idx[None, :] + 1.0) * log_gamma[:, None])[:, :, None]
    zeta = jnp.exp((chunk - 1.0 - idx)[None, :] * log_gamma[:, None])[:, :, None]
    gch = jnp.exp(chunk * log_gamma)[:, None, None]
    return decay, xi, zeta, gch


def _pad_heads(w, width):
    r, hn, n = w.shape
    return jnp.pad(w, ((0, 0), (0, 0), (0, width - n))).reshape(r, hn * width)


def _prep_layer(l, w_in, w_uq, w_uk, w_uv, w_pa, w_pb, w_out, mla_qa_g, mla_kva_g, mla_qn_g, mla_kn_g):
    d = w_in.shape[1]
    rqk, rvw = RET_HEADS * RET_DK, RET_HEADS * RET_DV
    q_rank, kv_rank = w_uq.shape[1], w_uk.shape[1]
    o = np.cumsum([0, rqk, rqk, rvw, rvw, q_rank, kv_rank, ROPE_DIM, d, d])
    wi = w_in[l]
    rq, rk, rv, rg, cq, ckv, kpe, ga, gb = [wi[:, o[i]:o[i + 1]] for i in range(9)]
    kpe_slot = jnp.pad(kpe, ((0, 0), (NOPE_DIM, LANES - QK_DIM)))
    w_a = jnp.concatenate([rv, rq, rk, cq, ckv, kpe_slot], axis=1).astype(BF16)
    offs = tuple(int(v) for v in np.cumsum([0, rvw, rqk, rqk, q_rank, kv_rank, LANES]))
    w_g = jnp.concatenate([rg, ga, gb], axis=1).astype(BF16)
    wuq = _pad_heads(w_uq[l], HEAD_SLOT).astype(BF16)
    wuk = _pad_heads(w_uk[l], HEAD_SLOT).astype(BF16)
    wuv = w_uv[l].reshape(kv_rank, MLA_HEADS * V_DIM).astype(BF16)
    wukt = jnp.pad(jnp.transpose(w_uk[l], (1, 2, 0)), ((0, 0), (0, HEAD_SLOT - NOPE_DIM), (0, 0))).astype(BF16)
    wuv_h = jnp.transpose(w_uv[l], (1, 0, 2)).astype(BF16)
    gqn, gkn = mla_qn_g[l], mla_kn_g[l]
    zpad = jnp.zeros((LANES - QK_DIM,), F32)
    aq = (jnp.concatenate([gqn[:NOPE_DIM] * gkn[:NOPE_DIM], gqn[NOPE_DIM:], zpad]) * (MLA_SCALE * LOG2E)).reshape(1, LANES)
    ak = jnp.concatenate([jnp.zeros((NOPE_DIM,), F32), gkn[NOPE_DIM:], zpad]).reshape(1, LANES)
    gains = (mla_qa_g[l].reshape(1, q_rank), mla_kva_g[l].reshape(1, kv_rank), aq, ak)
    return dict(w_a=w_a, offs=offs, w_g=w_g, wuq=wuq, wuk=wuk, wuv=wuv, wukt=wukt, wuv_h=wuv_h, gains=gains,
                wpa=w_pa[l].astype(BF16), wpb=w_pb[l].astype(BF16), wo=w_out[l].astype(BF16))


def _largest_tile(n, cap):
    t = min(n, cap)
    while n % t:
        t //= 2
    return t


def kernel(x_prompt, x_sample, cache_ckv, cache_kpe, cache_kscale, state_ret, page_table, c_prompt, c_sample, ada_w, ada_b, norm_g, ffn_w13, ffn_w2, w_in, ret_gn_g, mla_qa_g, mla_kva_g, w_uq, mla_qn_g, mla_kn_g, w_uk, w_uv, w_pa, w_pb, w_out):
    depth = ada_w.shape[0]
    bp, lp, d = x_prompt.shape
    ns, ls, _ = x_sample.shape
    assert ls == 1, "sample group is one new token per sequence"
    assert lp % RET_CHUNK == 0
    n_pages = page_table.shape[1]
    past_len = n_pages * PAGE_SIZE

    mod = _mod_call(jnp.concatenate([c_prompt, c_sample], axis=0), ada_w, ada_b)
    tabs_p = _rope_tables(jnp.arange(lp, dtype=F32))
    tabs_s = _rope_tables(jnp.full((ns,), float(past_len), F32))
    rt_p = _retention_tables(RET_CHUNK)
    rt_s = _retention_tables(1)

    tl_p = _largest_tile(lp, 512)
    tl_in = _largest_tile(lp, 256)
    tq = _largest_tile(lp, 512)
    bb = _largest_tile(ns, 8)
    nbt = _largest_tile(bp, 1)
    rcs = _largest_tile(lp // RET_CHUNK, 4)
    cp = _largest_tile(n_pages // 2, 64)
    kpe_pool_t = jnp.swapaxes(cache_kpe, 2, 3)
    ks_pool_t = jnp.swapaxes(cache_kscale, 2, 3)

    w13_all = ffn_w13.astype(BF16)
    w2_all = ffn_w2.astype(BF16)

    yp = x_prompt
    ys = x_sample.reshape(1, ns, d)
    st_p, st_s = [], []
    ret_s_new = []
    for l in range(depth):
        w = _prep_layer(l, w_in, w_uq, w_uk, w_uv, w_pa, w_pb, w_out, mla_qa_g, mla_kva_g, mla_qn_g, mla_kn_g)
        mod_p = mod[l, :bp].reshape(bp, 1, N_MOD * d)
        mod_s = mod[l, bp:].reshape(1, ns, N_MOD * d)
        ng = norm_g[l]

        yp = _ffn_call(yp, mod_p, 0, ng[0], w13_all, w2_all, l, 0, tl_p)
        rq, rk, rv, q, ckv, kpe, ks, kf, vf = _inproj_call(
            yp, mod_p, ng[1], w["w_a"], w["offs"], tabs_p, w["gains"], w["wuq"], w["wuk"], w["wuv"], tl_in, True)
        ret_o, ret_s = _ret_prompt_call(rq, rk, rv, rt_p, ret_gn_g[l], nbt, rcs)
        att_o = _attn_prompt_call(q, kf, vf, tq)
        yp = _outproj_call(yp, mod_p, ng[1], w["w_g"], ret_o, att_o, w["wpa"], w["wpb"], w["wo"], tl_p)
        yp = _ffn_call(yp, mod_p, 6, ng[2], w13_all, w2_all, l, 1, tl_p)
        st_p.append((ckv, kpe, ks, ret_s))

        ys = _ffn_call(ys, mod_s, 0, ng[0], w13_all, w2_all, l, 0, ns)
        rq, rk, rv, q, ckv, kpe, ks = _inproj_call(
            ys, mod_s, ng[1], w["w_a"], w["offs"], tabs_s, w["gains"], w["wuq"], w["wuk"], w["wuv"], ns, False)
        last = l == depth - 1
        ret_o, ret_s = _ret_sample_call(rq[0], rk[0], rv[0], state_ret, ret_s_new if last else [], l, rt_s,
                                        ret_gn_g[l], bb)
        ret_s_new.append(ret_s)
        qlat = _qlat_call(q[0], w["wukt"])

        olat = _attn_paged_call(page_table, qlat.reshape(ns, MLA_HEADS, -1), q[0].reshape(ns, MLA_HEADS, HEAD_SLOT),
                                ckv.reshape(ns, 1, -1), kpe.reshape(ns, 1, -1), ks.reshape(ns, MLA_HEADS, 1),
                                cache_ckv, kpe_pool_t, ks_pool_t, l, cp)
        att_o = _uv_call(olat.reshape(ns, -1), w["wuv_h"])
        ys = _outproj_call(ys, mod_s, ng[1], w["w_g"], ret_o[None], att_o[None], w["wpa"], w["wpb"], w["wo"], ns)
        ys = _ffn_call(ys, mod_s, 6, ng[2], w13_all, w2_all, l, 1, ns)
        st_s.append((ckv.reshape(ns, 1, -1), kpe.reshape(ns, 1, -1), ks.reshape(ns, 1, -1)))

    def stack(states, i):
        return jnp.stack([st[i] for st in states], axis=0)

    return (yp, ys.reshape(ns, 1, d),
            stack(st_p, 0), stack(st_p, 1), stack(st_p, 2), stack(st_p, 3),
            stack(st_s, 0), stack(st_s, 1), stack(st_s, 2),
            ret_s_new[-1] if depth > 1 else ret_s_new[-1][None])
```
